```python
import math
import jax, jax.numpy as jnp
from jax import lax
import numpy as np

D_MODEL = 4096
BATCH = 4
SEQ = 2048
DEPTH = 2
DEC_BATCH = 8
DEC_SEQ = 4
PAST_LEN = 16384
PAGE_SIZE = 128

N_EVEN = (DEPTH + 1) // 2
N_ODD = DEPTH // 2
A_HEADS = 16
A_HEAD_DIM = 128
A_WIDTH = A_HEADS * A_HEAD_DIM
MOBA_BLOCK = 256
MOBA_TOPK = 3
Q_CHUNK = 16
B_CHANNELS = D_MODEL // 2
B_CONV_WIDTH = 31
C_KEY_DIM = 128
C_HEADS = D_MODEL // C_KEY_DIM
C_VAL_DIM = D_MODEL // C_HEADS
C_WIDTH = C_HEADS * C_KEY_DIM
HG_CHUNK = 32
N_MEM = 256
MEM_HEADS = 4
MEM_HEAD_DIM = D_MODEL // MEM_HEADS
MEM_WIDTH = MEM_HEADS * MEM_HEAD_DIM
D_FF = ((8 * D_MODEL // 3 + 255) // 256) * 256
FFN_CONV_WIDTH = 3
EPS = 1e-6
NEG = -1e30

kernel_name = 'moba_conformer_hgrn2_convffn_memory_step'


def rmsnorm(x, g):
    x32 = x.astype(jnp.float32)
    y = x32 * lax.rsqrt(jnp.mean(x32 * x32, axis=-1, keepdims=True) + EPS)
    return (y * g.astype(jnp.float32)).astype(x.dtype)


def layernorm(x, g, b):
    x32 = x.astype(jnp.float32)
    mu = jnp.mean(x32, axis=-1, keepdims=True)
    xc = x32 - mu
    y = xc * lax.rsqrt(jnp.mean(xc * xc, axis=-1, keepdims=True) + EPS)
    return (y * g.astype(jnp.float32) + b.astype(jnp.float32)).astype(x.dtype)


def causal_dwconv(u, buf, w, b):
    C = u.shape[-1]
    xp = jnp.concatenate([buf.astype(u.dtype), u], axis=1)
    y = lax.conv_general_dilated(xp, w[:, None, :].astype(u.dtype), (1,), 'VALID',
                                 dimension_numbers=('NWC', 'WIO', 'NWC'), feature_group_count=C)
    return y + b.astype(u.dtype), xp[:, xp.shape[1] - (w.shape[0] - 1):]


def moba_prepare(k_all, v_all):
    B, L, H, Dh = k_all.shape
    Lp = -(-L // MOBA_BLOCK) * MOBA_BLOCK
    pad = ((0, 0), (0, Lp - L), (0, 0), (0, 0))
    nb = Lp // MOBA_BLOCK
    kb = jnp.pad(k_all, pad).reshape(B, nb, MOBA_BLOCK, H, Dh).transpose(0, 3, 1, 2, 4)
    vb = jnp.pad(v_all, pad).reshape(B, nb, MOBA_BLOCK, H, Dh).transpose(0, 3, 1, 2, 4)
    kmean = jnp.mean(kb.astype(jnp.float32), axis=3)
    return kb, vb, kmean


def moba_attend_chunk(q, q_pos, kb, vb, kmean):
    B, H, nb, BLK, Dh = kb.shape
    T = q.shape[1]
    q_blk = q_pos // MOBA_BLOCK
    gate = jnp.einsum('bthd,bhnd->bthn', q.astype(jnp.float32), kmean)
    is_past = jnp.arange(nb)[None, None, None, :] < q_blk[None, :, None, None]
    gate = jnp.where(is_past, gate, NEG)
    _, sel = lax.top_k(gate, min(MOBA_TOPK, nb))
    sel_ok = sel < q_blk[None, :, None, None]
    own = jnp.broadcast_to(q_blk[None, :, None, None], (B, T, H, 1)).astype(sel.dtype)
    blocks = jnp.concatenate([sel, own], axis=-1)
    slot_ok = jnp.concatenate([sel_ok, jnp.ones((B, T, H, 1), bool)], axis=-1)
    bi = jnp.arange(B)[:, None, None, None]
    hi = jnp.arange(H)[None, None, :, None]
    kg = kb[bi, hi, blocks]
    vg = vb[bi, hi, blocks]
    s = jnp.einsum('bthd,bthjkd->bthjk', q, kg, preferred_element_type=jnp.float32) / math.sqrt(Dh)
    key_pos = blocks[..., None] * MOBA_BLOCK + jnp.arange(BLK)
    mask = slot_ok[..., None] & (key_pos <= q_pos[None, :, None, None, None])
    s = jnp.where(mask, s, NEG)
    p = jax.nn.softmax(s.reshape(B, T, H, -1), axis=-1).reshape(s.shape)
    o = jnp.einsum('bthjk,bthjkd->bthd', p.astype(vg.dtype), vg, preferred_element_type=jnp.float32)
    return o.astype(q.dtype)


def moba(q, q_pos, k_all, v_all):
    kb, vb, kmean = moba_prepare(k_all, v_all)
    B, T, H, Dh = q.shape
    qc = min(Q_CHUNK, T)
    n = -(-T // qc)
    Tp = n * qc
    qp = jnp.pad(q, ((0, 0), (0, Tp - T), (0, 0), (0, 0)))
    posp = jnp.concatenate([q_pos, jnp.broadcast_to(q_pos[-1:], (Tp - T,))])
    qs = qp.reshape(B, n, qc, H, Dh).transpose(1, 0, 2, 3, 4)
    ps = posp.reshape(n, qc)
    out = lax.map(lambda a: moba_attend_chunk(a[0], a[1], kb, vb, kmean), (qs, ps))
    return out.transpose(1, 0, 2, 3, 4).reshape(B, Tp, H, Dh)[:, :T]


def mixer_moba_conv(h, w_in, w_out, conv_w, conv_b, ln_g, ln_b, k_past, v_past, conv_buf, q_pos):
    B, T, _ = h.shape
    z = h @ w_in
    q, k, v, glu = jnp.split(z, [A_WIDTH, 2 * A_WIDTH, 3 * A_WIDTH], axis=-1)
    q = q.reshape(B, T, A_HEADS, A_HEAD_DIM)
    k = k.reshape(B, T, A_HEADS, A_HEAD_DIM)
    v = v.reshape(B, T, A_HEADS, A_HEAD_DIM)
    if k_past is None:
        k_all, v_all = k, v
    else:
        k_all = jnp.concatenate([k_past.astype(k.dtype), k], axis=1)
        v_all = jnp.concatenate([v_past.astype(v.dtype), v], axis=1)
    o_a = moba(q, q_pos, k_all, v_all).reshape(B, T, A_WIDTH)
    ga, gb = jnp.split(glu, 2, axis=-1)
    u = ga * jax.nn.sigmoid(gb)
    c, new_buf = causal_dwconv(u, conv_buf, conv_w, conv_b)
    c = jax.nn.silu(layernorm(c, ln_g, ln_b))
    y = jnp.concatenate([o_a, c], axis=-1) @ w_out
    return y, k, v, new_buf


def hgrn2_scan(q, k, v, logf, S0):
    B, T, H, K = q.shape
    L = min(HG_CHUNK, T)
    n = -(-T // L)
    Tp = n * L
    def to_chunks(a):
        a = jnp.pad(a, ((0, 0), (0, Tp - T), (0, 0), (0, 0)))
        return a.reshape(B, n, L, H, a.shape[-1]).transpose(1, 0, 2, 3, 4)
    causal = jnp.tril(jnp.ones((L, L), bool))[None, :, :, None, None]
    def step(S, xs):
        qc, kc, vc, lc = xs
        b = jnp.cumsum(lc, axis=1)
        dec = jnp.exp(jnp.where(causal, b[:, :, None] - b[:, None, :], -jnp.inf))
        A = jnp.einsum('blhk,blmhk,bmhk->bhlm', qc, dec, kc)
        o = jnp.einsum('bhlm,bmhv->blhv', A, vc) + jnp.einsum('blhk,bhkv->blhv', qc * jnp.exp(b), S)
        bL = b[:, -1]
        S = jnp.exp(bL)[..., None] * S + jnp.einsum('bmhk,bmhv->bhkv', kc * jnp.exp(bL[:, None] - b), vc)
        return S, o
    S, o = lax.scan(step, S0, (to_chunks(q), to_chunks(k), to_chunks(v), to_chunks(logf)))
    o = o.transpose(1, 0, 2, 3, 4).reshape(B, Tp, H, v.shape[-1])[:, :T]
    return o, S


def mixer_hgrn2(h, w_in, w_out, lb, gnorm, S0):
    B, T, _ = h.shape
    z = h @ w_in
    q, f, i, g = jnp.split(z, 4, axis=-1)
    fg = lb + (1.0 - lb) * jax.nn.sigmoid(f.astype(jnp.float32))
    shp = (B, T, C_HEADS, C_KEY_DIM)
    o, S = hgrn2_scan(q.astype(jnp.float32).reshape(shp), (1.0 - fg).reshape(shp),
                      i.astype(jnp.float32).reshape(B, T, C_HEADS, C_VAL_DIM),
                      jnp.log(fg).reshape(shp), S0.astype(jnp.float32))
    o = rmsnorm(o, gnorm).reshape(B, T, C_HEADS * C_VAL_DIM).astype(h.dtype)
    y = (o * jax.nn.silu(g)) @ w_out
    return y, S


def mem_kv(mem, g, w_kv):
    B = mem.shape[0]
    kv = rmsnorm(mem, g) @ w_kv
    k, v = jnp.split(kv, 2, axis=-1)
    return (k.reshape(B, N_MEM, MEM_HEADS, MEM_HEAD_DIM), v.reshape(B, N_MEM, MEM_HEADS, MEM_HEAD_DIM))


def mem_attend(h, mk, mv, w_q, w_o):
    B, T, _ = h.shape
    q = (h @ w_q).reshape(B, T, MEM_HEADS, MEM_HEAD_DIM)
    s = jnp.einsum('bthd,bmhd->bhtm', q, mk.astype(q.dtype), preferred_element_type=jnp.float32) / math.sqrt(MEM_HEAD_DIM)
    p = jax.nn.softmax(s, axis=-1)
    o = jnp.einsum('bhtm,bmhd->bthd', p.astype(q.dtype), mv.astype(q.dtype))
    return o.reshape(B, T, MEM_WIDTH) @ w_o


def conv_ffn(h, w_in, conv_w, conv_b, w_out, buf):
    gate, up = jnp.split(h @ w_in, 2, axis=-1)
    gc, new_buf = causal_dwconv(gate, buf, conv_w, conv_b)
    return (jax.nn.silu(gc) * up) @ w_out, new_buf


def setup_inputs(seed: int = 0) -> dict:
    key = jax.random.key(seed)
    keys = iter(jax.random.split(key, 64))
    f32 = jnp.float32
    def nrm(shape, scale=1.0):
        return scale * jax.random.normal(next(keys), shape, f32)
    def gain(shape):
        return 1.0 + 0.02 * nrm(shape)
    D = D_MODEL
    n_pages = PAST_LEN // PAGE_SIZE
    n_used = DEC_BATCH * n_pages
    n_pool = n_used + (n_used + 3) // 4
    page_table = jax.random.permutation(next(keys), n_pool)[:n_used].reshape(DEC_BATCH, n_pages).astype(jnp.int32)
    in_a = 3 * A_WIDTH + 2 * B_CHANNELS
    out_a = A_WIDTH + B_CHANNELS
    return {
        'x_prompt': nrm((BATCH, SEQ, D)),
        'x_sample': nrm((DEC_BATCH, DEC_SEQ, D)),
        'mem_prompt': nrm((BATCH, N_MEM, D)),
        'cache_moba_k': nrm((N_EVEN, n_pool, PAGE_SIZE, A_HEADS, A_HEAD_DIM)),
        'cache_moba_v': nrm((N_EVEN, n_pool, PAGE_SIZE, A_HEADS, A_HEAD_DIM)),
        'page_table': page_table,
        'cache_convmod': nrm((N_EVEN, DEC_BATCH, B_CONV_WIDTH - 1, B_CHANNELS), 0.5),
        'state_hgrn': nrm((N_ODD, DEC_BATCH, C_HEADS, C_KEY_DIM, C_VAL_DIM), 0.3),
        'cache_mem_k': nrm((DEPTH, DEC_BATCH, N_MEM, MEM_HEADS, MEM_HEAD_DIM)),
        'cache_mem_v': nrm((DEPTH, DEC_BATCH, N_MEM, MEM_HEADS, MEM_HEAD_DIM)),
        'cache_ffn_conv': nrm((DEPTH, DEC_BATCH, FFN_CONV_WIDTH - 1, D_FF)),
        'g_mix': gain((DEPTH, D)),
        'g_mem_q': gain((DEPTH, D)),
        'g_mem_kv': gain((DEPTH, D)),
        'g_ffn': gain((DEPTH, D)),
        'g_final': gain((D,)),
        'w_in_a': nrm((N_EVEN, D, in_a), D ** -0.5),
        'w_out_a': nrm((N_EVEN, out_a, D), out_a ** -0.5),
        'conv_w_b': nrm((N_EVEN, B_CONV_WIDTH, B_CHANNELS), B_CONV_WIDTH ** -0.5),
        'conv_b_b': nrm((N_EVEN, B_CHANNELS), 0.01),
        'ln_g_b': gain((N_EVEN, B_CHANNELS)),
        'ln_b_b': nrm((N_EVEN, B_CHANNELS), 0.01),
        'w_in_c': nrm((N_ODD, D, 4 * C_WIDTH), D ** -0.5),
        'w_out_c': nrm((N_ODD, C_HEADS * C_VAL_DIM, D), (C_HEADS * C_VAL_DIM) ** -0.5),
        'lower_bounds': nrm((DEPTH, C_WIDTH), 0.1),
        'gnorm_c': gain((N_ODD, C_VAL_DIM)),
        'w_mem_q': nrm((DEPTH, D, MEM_WIDTH), D ** -0.5),
        'w_mem_kv': nrm((DEPTH, D, 2 * MEM_WIDTH), D ** -0.5),
        'w_mem_o': nrm((DEPTH, MEM_WIDTH, D), MEM_WIDTH ** -0.5),
        'w_ffn_in': nrm((DEPTH, D, 2 * D_FF), D ** -0.5),
        'ffn_conv_w': nrm((DEPTH, FFN_CONV_WIDTH, D_FF), FFN_CONV_WIDTH ** -0.5),
        'ffn_conv_b': nrm((DEPTH, D_FF), 0.01),
        'w_ffn_out': nrm((DEPTH, D_FF, D), D_FF ** -0.5),
    }


def reference(x_prompt, x_sample, mem_prompt, cache_moba_k, cache_moba_v, page_table, cache_convmod,
              state_hgrn, cache_mem_k, cache_mem_v, cache_ffn_conv, g_mix, g_mem_q, g_mem_kv, g_ffn, g_final,
              w_in_a, w_out_a, conv_w_b, conv_b_b, ln_g_b, ln_b_b, w_in_c, w_out_c, lower_bounds, gnorm_c,
              w_mem_q, w_mem_kv, w_mem_o, w_ffn_in, ffn_conv_w, ffn_conv_b, w_ffn_out):
    Bp, Tp, _ = x_prompt.shape
    Bs, Ts, _ = x_sample.shape
    n_pages = page_table.shape[1]
    past_len = n_pages * PAGE_SIZE
    pos_p = jnp.arange(Tp, dtype=jnp.int32)
    pos_s = past_len + jnp.arange(Ts, dtype=jnp.int32)
    sm = jax.nn.softmax(lower_bounds.astype(jnp.float32), axis=0)
    lbs = jnp.cumsum(sm, axis=0) - sm[0]
    xp, xs = x_prompt, x_sample
    mk_p, mv_p, ms_p, ms_s, cv_p, cv_s, hg_p, hg_s = [], [], [], [], [], [], [], []
    mem_k_p, mem_v_p, ff_p, ff_s = [], [], [], []
    for l in range(DEPTH):
        j = l // 2
        hp = rmsnorm(xp, g_mix[l])
        hs = rmsnorm(xs, g_mix[l])
        if l % 2 == 0:
            zero_buf = jnp.zeros((Bp, B_CONV_WIDTH - 1, B_CHANNELS), xp.dtype)
            yp, kp, vp, bp = mixer_moba_conv(hp, w_in_a[j], w_out_a[j], conv_w_b[j], conv_b_b[j], ln_g_b[j],
                                             ln_b_b[j], None, None, zero_buf, pos_p)
            k_past = cache_moba_k[j][page_table].reshape(Bs, past_len, A_HEADS, A_HEAD_DIM)
            v_past = cache_moba_v[j][page_table].reshape(Bs, past_len, A_HEADS, A_HEAD_DIM)
            ys, ks, vs, bs = mixer_moba_conv(hs, w_in_a[j], w_out_a[j], conv_w_b[j], conv_b_b[j], ln_g_b[j],
                                             ln_b_b[j], k_past, v_past, cache_convmod[j], pos_s)
            mk_p.append(kp); mv_p.append(vp); ms_p.append(ks); ms_s.append(vs)
            cv_p.append(bp); cv_s.append(bs)
        else:
            S0 = jnp.zeros((Bp, C_HEADS, C_KEY_DIM, C_VAL_DIM), jnp.float32)
            yp, Sp = mixer_hgrn2(hp, w_in_c[j], w_out_c[j], lbs[l], gnorm_c[j], S0)
            ys, Ss = mixer_hgrn2(hs, w_in_c[j], w_out_c[j], lbs[l], gnorm_c[j], state_hgrn[j])
            hg_p.append(Sp); hg_s.append(Ss)
        xp = xp + yp
        xs = xs + ys
        mkp, mvp = mem_kv(mem_prompt, g_mem_kv[l], w_mem_kv[l])
        xp = xp + mem_attend(rmsnorm(xp, g_mem_q[l]), mkp, mvp, w_mem_q[l], w_mem_o[l])
        xs = xs + mem_attend(rmsnorm(xs, g_mem_q[l]), cache_mem_k[l], cache_mem_v[l], w_mem_q[l], w_mem_o[l])
        mem_k_p.append(mkp); mem_v_p.append(mvp)
        zero_ff = jnp.zeros((Bp, FFN_CONV_WIDTH - 1, D_FF), xp.dtype)
        fp, fbp = conv_ffn(rmsnorm(xp, g_ffn[l]), w_ffn_in[l], ffn_conv_w[l], ffn_conv_b[l], w_ffn_out[l], zero_ff)
        fs, fbs = conv_ffn(rmsnorm(xs, g_ffn[l]), w_ffn_in[l], ffn_conv_w[l], ffn_conv_b[l], w_ffn_out[l], cache_ffn_conv[l])
        xp = xp + fp
        xs = xs + fs
        ff_p.append(fbp); ff_s.append(fbs)
    y_prompt = rmsnorm(xp, g_final)
    y_sample = rmsnorm(xs, g_final)
    return (y_prompt, y_sample, jnp.stack(mk_p), jnp.stack(mv_p), jnp.stack(ms_p), jnp.stack(ms_s),
            jnp.stack(cv_p), jnp.stack(cv_s), jnp.stack(hg_p), jnp.stack(hg_s),
            jnp.stack(mem_k_p), jnp.stack(mem_v_p), jnp.stack(ff_p), jnp.stack(ff_s))
```

```python
import functools
import math

import jax
import jax.numpy as jnp
from jax import lax
from jax.experimental import pallas as pl
from jax.experimental.pallas import tpu as pltpu

F32 = jnp.float32
BF16 = jnp.bfloat16

EPS = 1e-6
NEG = -1e30

PAGE_SIZE = 128
A_HEADS = 16
A_HEAD_DIM = 128
A_WIDTH = A_HEADS * A_HEAD_DIM
MOBA_BLOCK = 256
MOBA_TOPK = 3
B_CONV_WIDTH = 31
C_KEY_DIM = 128
MEM_HEADS = 4
FFN_CONV_WIDTH = 3

LANES = 128
BF16_SUBLANES = 16
VMEM_LIMIT = 56 * 1024 * 1024

NT_DIMS = (((1,), (1,)), ((), ()))
TN_DIMS = (((0,), (0,)), ((), ()))


def _params(sem, vmem=VMEM_LIMIT):
    return pltpu.CompilerParams(dimension_semantics=sem, vmem_limit_bytes=vmem)


def _sigmoid(x):
    return 1.0 / (1.0 + jnp.exp(-x))


def _split3(a):
    a1 = a.astype(BF16)
    r1 = a - a1.astype(F32)
    a2 = r1.astype(BF16)
    a3 = (r1 - a2.astype(F32)).astype(BF16)
    return a1, a2, a3


def _dot_f32(a, b, dims):
    a1, a2, a3 = _split3(a)
    b1, b2, b3 = _split3(b)
    d = lambda x, y: lax.dot_general(x, y, dims, preferred_element_type=F32)
    low = d(a1, b3) + d(a3, b1) + d(a2, b2)
    mid = d(a1, b2) + d(a2, b1)
    return (low + mid) + d(a1, b1)


def _rmsnorm_kernel(x_ref, g_ref, o_ref):
    x = x_ref[...]
    ms = jnp.mean(x * x, axis=-1, keepdims=True)
    o_ref[...] = (x * lax.rsqrt(ms + EPS) * g_ref[...]).astype(o_ref.dtype)


def _rmsnorm(x, g, out_dtype):
    m, d = x.shape
    tm = min(m, 512)
    return pl.pallas_call(
        _rmsnorm_kernel,
        grid=(m // tm,),
        in_specs=[pl.BlockSpec((tm, d), lambda i: (i, 0)),
                  pl.BlockSpec((1, d), lambda i: (0, 0))],
        out_specs=pl.BlockSpec((tm, d), lambda i: (i, 0)),
        out_shape=jax.ShapeDtypeStruct((m, d), out_dtype),
        compiler_params=_params(("parallel",)),
        name="rmsnorm",
    )(x, g.reshape(1, d))


def _mm_kernel(x_ref, w_ref, *rest, nk, has_res):
    if has_res:
        r_ref, o_ref = rest
    else:
        (o_ref,) = rest
    acc = jnp.dot(x_ref[...].astype(BF16), w_ref[...], preferred_element_type=F32)
    if nk == 1:
        if has_res:
            acc = acc + r_ref[...]
        o_ref[...] = acc.astype(o_ref.dtype)
    else:
        k = pl.program_id(2)

        @pl.when(k == 0)
        def _():
            o_ref[...] = (acc + r_ref[...]) if has_res else acc

        @pl.when(k > 0)
        def _():
            o_ref[...] += acc


def _mm(x, w, *, tm, tn, tk=None, n_off=0, n=None, res=None, out_dtype=F32, name="matmul"):
    m, kdim = x.shape
    n = w.shape[1] if n is None else n
    tk = kdim if tk is None else tk
    nk = kdim // tk
    assert m % tm == 0 and n % tn == 0 and kdim % tk == 0 and n_off % tn == 0
    assert nk == 1 or out_dtype == F32
    joff = n_off // tn
    in_specs = [pl.BlockSpec((tm, tk), lambda i, j, k: (i, k)),
                pl.BlockSpec((tk, tn), lambda i, j, k: (k, j + joff))]
    args = [x, w]
    if res is not None:
        in_specs.append(pl.BlockSpec((tm, tn), lambda i, j, k: (i, j)))
        args.append(res)
    return pl.pallas_call(
        functools.partial(_mm_kernel, nk=nk, has_res=res is not None),
        grid=(m // tm, n // tn, nk),
        in_specs=in_specs,
        out_specs=pl.BlockSpec((tm, tn), lambda i, j, k: (i, j)),
        out_shape=jax.ShapeDtypeStruct((m, n), out_dtype),
        compiler_params=_params(("parallel", "parallel", "arbitrary")),
        name=name,
    )(*args)


def _moba_prompt_kernel(q_ref, k_ref, v_ref, o_ref, *, nblk):
    blk = MOBA_BLOCK
    scale = 1.0 / math.sqrt(A_HEAD_DIM)
    k = k_ref[...]
    kb = k.astype(BF16)
    vb = v_ref[...].astype(BF16)
    kmean = jnp.mean(k.reshape(nblk, blk, A_HEAD_DIM), axis=1)
    row = lax.broadcasted_iota(jnp.int32, (blk, blk), 0)
    col = lax.broadcasted_iota(jnp.int32, (blk, blk), 1)
    causal = col <= row
    for qb in range(nblk):
        qf = q_ref[qb * blk:(qb + 1) * blk, :]
        qbf = qf.astype(BF16)
        sel = None
        if qb > MOBA_TOPK:
            gate = _dot_f32(qf, kmean[:qb], NT_DIMS)
            cols = [gate[:, n:n + 1] for n in range(qb)]
            sel = []
            for n in range(qb):
                rank = jnp.zeros((blk, 1), jnp.int32)
                for n2 in range(qb):
                    if n2 == n:
                        continue
                    beats = (cols[n2] >= cols[n]) if n2 < n else (cols[n2] > cols[n])
                    rank = rank + beats.astype(jnp.int32)
                sel.append(rank < MOBA_TOPK)
        s_list = []
        for n in range(qb + 1):
            s = lax.dot_general(qbf, kb[n * blk:(n + 1) * blk], NT_DIMS,
                                preferred_element_type=F32) * scale
            if n == qb:
                s = jnp.where(causal, s, NEG)
            elif sel is not None:
                s = jnp.where(sel[n], s, NEG)
            s_list.append(s)
        m = s_list[0].max(axis=1, keepdims=True)
        for s in s_list[1:]:
            m = jnp.maximum(m, s.max(axis=1, keepdims=True))
        l = jnp.zeros((blk, 1), F32)
        o = jnp.zeros((blk, A_HEAD_DIM), F32)
        for n, s in enumerate(s_list):
            p = jnp.exp(s - m)
            l = l + p.sum(axis=1, keepdims=True)
            o = o + jnp.dot(p.astype(BF16), vb[n * blk:(n + 1) * blk],
                            preferred_element_type=F32)
        o_ref[qb * blk:(qb + 1) * blk, :] = (o / l).astype(o_ref.dtype)


def _moba_prompt(z, nb, t):
    hd = A_HEAD_DIM
    return pl.pallas_call(
        functools.partial(_moba_prompt_kernel, nblk=t // MOBA_BLOCK),
        grid=(nb, A_HEADS),
        in_specs=[pl.BlockSpec((t, hd), lambda b, h: (b, h)),
                  pl.BlockSpec((t, hd), lambda b, h: (b, A_HEADS + h)),
                  pl.BlockSpec((t, hd), lambda b, h: (b, 2 * A_HEADS + h))],
        out_specs=pl.BlockSpec((t, hd), lambda b, h: (b, h)),
        out_shape=jax.ShapeDtypeStruct((nb * t, A_WIDTH), BF16),
        compiler_params=_params(("parallel", "parallel")),
        name="moba_prompt",
    )(z, z, z)


def _kmean_kernel(pt_ref, p0_ref, p1_ref, o_ref):
    s = jnp.sum(p0_ref[0], axis=0, keepdims=True) + jnp.sum(p1_ref[0], axis=0, keepdims=True)
    o_ref[0, 0] = s * (1.0 / MOBA_BLOCK)


def _moba_kmean(cache_k, pt_flat, nb, n_pages):
    nblk = n_pages * PAGE_SIZE // MOBA_BLOCK
    page = lambda pg: pl.BlockSpec(
        (1, PAGE_SIZE, A_WIDTH), lambda b, n, pt: (pt[b * n_pages + 2 * n + pg], 0, 0))
    return pl.pallas_call(
        _kmean_kernel,
        grid_spec=pltpu.PrefetchScalarGridSpec(
            num_scalar_prefetch=1,
            grid=(nb, nblk),
            in_specs=[page(0), page(1)],
            out_specs=pl.BlockSpec((1, 1, 1, A_WIDTH), lambda b, n, pt: (b, n, 0, 0)),
        ),
        out_shape=jax.ShapeDtypeStruct((nb, nblk, 1, A_WIDTH), F32),
        compiler_params=_params(("parallel", "parallel")),
        name="moba_kmean",
    )(pt_flat, cache_k, cache_k)


def _moba_select_kernel(q_ref, km_ref, o_ref, *, ts, nblk):
    q = q_ref[0]
    km = km_ref[0]
    nq = ts * A_HEADS
    qrep = jnp.concatenate(
        [jnp.broadcast_to(q[t:t + 1], (A_HEADS, A_WIDTH)) for t in range(ts)], axis=0)
    rowh = lax.broadcasted_iota(jnp.int32, (nq, A_WIDTH), 0) % A_HEADS
    colh = lax.broadcasted_iota(jnp.int32, (nq, A_WIDTH), 1) // A_HEAD_DIM
    qbig = jnp.where(rowh == colh, qrep, 0.0)
    gate = _dot_f32(km, qbig, NT_DIMS)
    nidx = lax.broadcasted_iota(jnp.int32, (nblk, nq), 0)
    for r in range(MOBA_TOPK):
        m = gate.max(axis=0, keepdims=True)
        idx = jnp.min(jnp.where(gate == m, nidx, nblk), axis=0, keepdims=True)
        o_ref[0, r:r + 1, :] = idx
        gate = jnp.where(nidx == idx, -jnp.inf, gate)


def _moba_select(zs3, kmean, ts):
    nb, nblk, _ = kmean.shape
    nq = ts * A_HEADS
    return pl.pallas_call(
        functools.partial(_moba_select_kernel, ts=ts, nblk=nblk),
        grid=(nb,),
        in_specs=[pl.BlockSpec((1, ts, A_WIDTH), lambda b: (b, 0, 0)),
                  pl.BlockSpec((1, nblk, A_WIDTH), lambda b: (b, 0, 0))],
        out_specs=pl.BlockSpec((1, MOBA_TOPK, nq), lambda b: (b, 0, 0)),
        out_shape=jax.ShapeDtypeStruct((nb, MOBA_TOPK, nq), jnp.int32),
        compiler_params=_params(("parallel",)),
        name="moba_select",
    )(zs3, kmean)


def _moba_sample_kernel(sel_ref, pt_ref, q_ref, kn_ref, vn_ref, *rest, ts, npg):
    k_refs = rest[:npg]
    v_refs = rest[npg:2 * npg]
    o_ref = rest[2 * npg]
    t = pl.program_id(1)
    scale = 1.0 / math.sqrt(A_HEAD_DIM)
    q = q_ref[0, 0] * scale
    s_cols = [jnp.sum(kr[0] * q, axis=1, keepdims=True) for kr in k_refs]
    s_own = jnp.sum(kn_ref[0] * q, axis=1, keepdims=True)
    own_ok = lax.broadcasted_iota(jnp.int32, (ts, 1), 0) <= t
    s_own = jnp.where(own_ok, s_own, NEG)
    m = s_own.max(axis=0, keepdims=True)
    for s in s_cols:
        m = jnp.maximum(m, s.max(axis=0, keepdims=True))
    p_own = jnp.exp(s_own - m)
    l = p_own.sum(axis=0, keepdims=True)
    o = jnp.sum(p_own * vn_ref[0], axis=0, keepdims=True)
    for s, vr in zip(s_cols, v_refs):
        p = jnp.exp(s - m)
        l = l + p.sum(axis=0, keepdims=True)
        o = o + jnp.sum(p * vr[0], axis=0, keepdims=True)
    o_ref[0, 0] = o / l


def _moba_sample(zs3, zs4, cache_k, cache_v, sel_flat, pt_flat, nb, ts, n_pages):
    hd = A_HEAD_DIM
    nblk = n_pages * PAGE_SIZE // MOBA_BLOCK
    nq = ts * A_HEADS
    pages_per_blk = MOBA_BLOCK // PAGE_SIZE
    npg = MOBA_TOPK * pages_per_blk

    def page_spec(r, pg):
        def imap(b, t, h, sel, pt):
            blkid = jnp.minimum(sel[b * (MOBA_TOPK * nq) + r * nq + t * A_HEADS + h], nblk - 1)
            return (pt[b * n_pages + blkid * pages_per_blk + pg], 0, h)
        return pl.BlockSpec((1, PAGE_SIZE, hd), imap)

    pages = [page_spec(r, pg) for r in range(MOBA_TOPK) for pg in range(pages_per_blk)]
    return pl.pallas_call(
        functools.partial(_moba_sample_kernel, ts=ts, npg=npg),
        grid_spec=pltpu.PrefetchScalarGridSpec(
            num_scalar_prefetch=2,
            grid=(nb, ts, A_HEADS),
            in_specs=[pl.BlockSpec((1, 1, 1, hd), lambda b, t, h, sel, pt: (b, t, 0, h)),
                      pl.BlockSpec((1, ts, hd), lambda b, t, h, sel, pt: (b, 0, A_HEADS + h)),
                      pl.BlockSpec((1, ts, hd), lambda b, t, h, sel, pt: (b, 0, 2 * A_HEADS + h))]
                     + pages + pages,
            out_specs=pl.BlockSpec((1, 1, 1, hd), lambda b, t, h, sel, pt: (b, t, 0, h)),
        ),
        out_shape=jax.ShapeDtypeStruct((nb, ts, 1, A_WIDTH), F32),
        compiler_params=_params(("parallel", "parallel", "parallel")),
        name="moba_sample",
    )(sel_flat, pt_flat, zs4, zs3, zs3, *([cache_k] * npg), *([cache_v] * npg))


def _layernorm_swish(y, g, b):
    mu = jnp.mean(y, axis=-1, keepdims=True)
    yc = y - mu
    var = jnp.mean(yc * yc, axis=-1, keepdims=True)
    yn = yc * lax.rsqrt(var + EPS) * g + b
    return yn * _sigmoid(yn)


CONV_HALO = 32
CONV_ROWS = 32
CONV_COLS = 256


def _convmod_prompt_kernel(ga_ref, gb_ref, cw_ref, cb_ref, lg_ref, lb_ref, c_ref, nb_ref,
                           ubuf, ybuf, *, tt, nt):
    w = B_CONV_WIDTH
    ch = ubuf.shape[1]
    t = pl.program_id(1)

    @pl.when(t == 0)
    def _():
        ubuf[0:CONV_HALO, :] = jnp.zeros((CONV_HALO, ch), F32)

    ubuf[CONV_HALO:CONV_HALO + tt, :] = ga_ref[...] * _sigmoid(gb_ref[...])
    base = CONV_HALO - (w - 1)
    for c0 in range(0, ch, CONV_COLS):
        for r0 in range(0, tt, CONV_ROWS):
            acc = jnp.broadcast_to(cb_ref[:, c0:c0 + CONV_COLS], (CONV_ROWS, CONV_COLS))
            for j in range(w):
                acc = acc + cw_ref[j:j + 1, c0:c0 + CONV_COLS] * \
                    ubuf[r0 + j + base:r0 + j + base + CONV_ROWS, c0:c0 + CONV_COLS]
            ybuf[r0:r0 + CONV_ROWS, c0:c0 + CONV_COLS] = acc
    c_ref[...] = _layernorm_swish(ybuf[...], lg_ref[...], lb_ref[...]).astype(c_ref.dtype)

    @pl.when(t == nt - 1)
    def _():
        nb_ref[0] = ubuf[CONV_HALO + tt - (w - 1):CONV_HALO + tt, :]

    ubuf[0:CONV_HALO, :] = ubuf[tt:tt + CONV_HALO, :]


def _convmod_prompt(z, nb, t, col0, ch, cw, cb, lg, lb):
    tt = 256
    nt = t // tt
    cb0 = col0 // ch
    vec = lambda: pl.BlockSpec((1, ch), lambda b, i: (0, 0))
    return pl.pallas_call(
        functools.partial(_convmod_prompt_kernel, tt=tt, nt=nt),
        grid=(nb, nt),
        in_specs=[pl.BlockSpec((tt, ch), lambda b, i: (b * nt + i, cb0)),
                  pl.BlockSpec((tt, ch), lambda b, i: (b * nt + i, cb0 + 1)),
                  pl.BlockSpec((B_CONV_WIDTH, ch), lambda b, i: (0, 0)),
                  vec(), vec(), vec()],
        out_specs=[pl.BlockSpec((tt, ch), lambda b, i: (b * nt + i, 0)),
                   pl.BlockSpec((1, B_CONV_WIDTH - 1, ch), lambda b, i: (b, 0, 0))],
        out_shape=[jax.ShapeDtypeStruct((nb * t, ch), BF16),
                   jax.ShapeDtypeStruct((nb, B_CONV_WIDTH - 1, ch), F32)],
        scratch_shapes=[pltpu.VMEM((CONV_HALO + tt, ch), F32), pltpu.VMEM((tt, ch), F32)],
        compiler_params=_params(("parallel", "arbitrary")),
        name="convmod_prompt",
    )(z, z, cw, cb.reshape(1, ch), lg.reshape(1, ch), lb.reshape(1, ch))


def _convmod_sample_kernel(ga_ref, gb_ref, cache_ref, cw_ref, cb_ref, lg_ref, lb_ref,
                           c_ref, nb_ref, ubuf, *, ts):
    w = B_CONV_WIDTH
    ch = ubuf.shape[1]
    ubuf[0:w - 1, :] = cache_ref[0]
    ubuf[w - 1:w - 1 + ts, :] = ga_ref[0] * _sigmoid(gb_ref[0])
    acc = jnp.broadcast_to(cb_ref[...], (ts, ch))
    for j in range(w):
        acc = acc + cw_ref[j:j + 1, :] * ubuf[j:j + ts, :]
    c_ref[0] = _layernorm_swish(acc, lg_ref[...], lb_ref[...])
    nb_ref[0] = ubuf[ts:ts + w - 1, :]


def _convmod_sample(zs3, cache, col0, ch, cw, cb, lg, lb):
    nb, ts, _ = zs3.shape
    w = B_CONV_WIDTH
    cb0 = col0 // ch
    vec = lambda: pl.BlockSpec((1, ch), lambda b: (0, 0))
    return pl.pallas_call(
        functools.partial(_convmod_sample_kernel, ts=ts),
        grid=(nb,),
        in_specs=[pl.BlockSpec((1, ts, ch), lambda b: (b, 0, cb0)),
                  pl.BlockSpec((1, ts, ch), lambda b: (b, 0, cb0 + 1)),
                  pl.BlockSpec((1, w - 1, ch), lambda b: (b, 0, 0)),
                  pl.BlockSpec((w, ch), lambda b: (0, 0)),
                  vec(), vec(), vec()],
        out_specs=[pl.BlockSpec((1, ts, ch), lambda b: (b, 0, 0)),
                   pl.BlockSpec((1, w - 1, ch), lambda b: (b, 0, 0))],
        out_shape=[jax.ShapeDtypeStruct((nb, ts, ch), F32),
                   jax.ShapeDtypeStruct((nb, w - 1, ch), F32)],
        scratch_shapes=[pltpu.VMEM((w - 1 + ts + 5, ch), F32)],
        compiler_params=_params(("parallel",)),
        name="convmod_sample",
    )(zs3, zs3, cache, cw, cb.reshape(1, ch), lg.reshape(1, ch), lb.reshape(1, ch))


HG_CHUNK = 128
HG_SUB = 16


def _hgrn_lower_bound(lbp, layer):
    e = jnp.exp(lbp - lbp.max(axis=0, keepdims=True))
    sm = e / e.sum(axis=0, keepdims=True)
    lb = jnp.zeros_like(sm[0:1])
    for i in range(1, layer + 1):
        lb = lb + sm[i:i + 1]
    return lb


def _hgrn_out(o, g, gn):
    ms = jnp.mean(o * o, axis=-1, keepdims=True)
    return (o * lax.rsqrt(ms + EPS) * gn) * (g * _sigmoid(g))


def _hgrn_prompt_kernel(q_ref, f_ref, i_ref, g_ref, lbp_ref, gn_ref, o_ref, s_ref,
                        s_scr, kpad, bpad, *, layer, nchunk):
    c = pl.program_id(2)
    cs, sub, kd = HG_CHUNK, HG_SUB, C_KEY_DIM

    @pl.when(c == 0)
    def _():
        s_scr[...] = jnp.zeros_like(s_scr)

    lb = _hgrn_lower_bound(lbp_ref[...], layer)
    fg = lb + (1.0 - lb) * _sigmoid(f_ref[...])
    kk = 1.0 - fg
    q = q_ref[...]
    v = i_ref[...]
    vb = v.astype(BF16)

    rr = lax.broadcasted_iota(jnp.int32, (cs, cs), 0)
    cc = lax.broadcasted_iota(jnp.int32, (cs, cs), 1)
    tril = jnp.where(cc <= rr, 1.0, 0.0).astype(BF16)
    l1, l2, l3 = _split3(jnp.log(fg))
    cum = lambda x: jnp.dot(tril, x, preferred_element_type=F32)
    b = (cum(l3) + cum(l2)) + cum(l1)

    s_prev = s_scr[...]
    o = jnp.dot((q * jnp.exp(b)).astype(BF16), s_prev.astype(BF16), preferred_element_type=F32)

    bref_rows = jnp.concatenate(
        [jnp.zeros((sub, kd), F32)]
        + [jnp.broadcast_to(b[sub * i - 1:sub * i], (sub, kd)) for i in range(1, cs // sub)], axis=0)
    qe = (q * jnp.exp(b - bref_rows)).astype(BF16)
    rowk = lax.broadcasted_iota(jnp.int32, (cs, kd), 0)
    parts = [jnp.zeros((sub, cs), F32)]
    for i in range(1, cs // sub):
        bref = b[sub * i - 1:sub * i]
        ke = jnp.where(rowk < sub * i, kk * jnp.exp(jnp.minimum(bref - b, 0.0)), 0.0).astype(BF16)
        parts.append(lax.dot_general(qe[sub * i:sub * (i + 1)], ke, NT_DIMS,
                                     preferred_element_type=F32))
    a = jnp.concatenate(parts, axis=0)

    kpad[0:sub, :] = jnp.zeros((sub, kd), F32)
    bpad[0:sub, :] = jnp.zeros((sub, kd), F32)
    kpad[sub:, :] = kk
    bpad[sub:, :] = b
    rmod = rr % sub
    for d in range(sub):
        if d == 0:
            p = q * kk
        else:
            ks = kpad[sub - d:sub - d + cs, :]
            bs = bpad[sub - d:sub - d + cs, :]
            p = q * ks * jnp.exp(jnp.minimum(b - bs, 0.0))
        r = p.sum(axis=1, keepdims=True)
        a = a + jnp.where((cc == rr - d) & (rmod >= d), r, 0.0)
    o = o + jnp.dot(a.astype(BF16), vb, preferred_element_type=F32)

    bl = b[cs - 1:cs]
    kd_mat = (kk * jnp.exp(bl - b)).astype(BF16)
    decay_col = jnp.sum(jnp.where(rr == cc, jnp.broadcast_to(jnp.exp(bl), (kd, kd)), 0.0),
                        axis=1, keepdims=True)
    s_new = decay_col * s_prev + lax.dot_general(kd_mat, vb, TN_DIMS, preferred_element_type=F32)
    s_scr[...] = s_new

    o_ref[...] = _hgrn_out(o, g_ref[...], gn_ref[...]).astype(o_ref.dtype)

    @pl.when(c == nchunk - 1)
    def _():
        s_ref[0, 0] = s_new


def _hgrn_prompt(z, nb, t, heads, lower_bounds, gnorm, layer):
    kd = C_KEY_DIM
    assert HG_CHUNK == kd
    nchunk = t // HG_CHUNK
    depth = lower_bounds.shape[0]
    col = lambda part: pl.BlockSpec((HG_CHUNK, kd), lambda b, h, c: (b * nchunk + c, part * heads + h))
    return pl.pallas_call(
        functools.partial(_hgrn_prompt_kernel, layer=layer, nchunk=nchunk),
        grid=(nb, heads, nchunk),
        in_specs=[col(0), col(1), col(2), col(3),
                  pl.BlockSpec((depth, kd), lambda b, h, c: (0, h)),
                  pl.BlockSpec((1, kd), lambda b, h, c: (0, 0))],
        out_specs=[pl.BlockSpec((HG_CHUNK, kd), lambda b, h, c: (b * nchunk + c, h)),
                   pl.BlockSpec((1, 1, kd, kd), lambda b, h, c: (b, h, 0, 0))],
        out_shape=[jax.ShapeDtypeStruct((nb * t, heads * kd), BF16),
                   jax.ShapeDtypeStruct((nb, heads, kd, kd), F32)],
        scratch_shapes=[pltpu.VMEM((kd, kd), F32),
                        pltpu.VMEM((HG_SUB + HG_CHUNK, kd), F32),
                        pltpu.VMEM((HG_SUB + HG_CHUNK, kd), F32)],
        compiler_params=_params(("parallel", "parallel", "arbitrary")),
        name="hgrn_prompt",
    )(z, z, z, z, lower_bounds, gnorm.reshape(1, kd))


def _hgrn_sample_kernel(q_ref, f_ref, i_ref, g_ref, lbp_ref, gn_ref, s0_ref, o_ref, s_ref,
                        *, layer, ts):
    kd = C_KEY_DIM
    lb = _hgrn_lower_bound(lbp_ref[...], layer)
    fg = lb + (1.0 - lb) * _sigmoid(f_ref[0])
    kk = 1.0 - fg
    q = q_ref[0]
    v = i_ref[0]
    g = g_ref[0]
    eye = lax.broadcasted_iota(jnp.int32, (kd, kd), 0) == lax.broadcasted_iota(jnp.int32, (kd, kd), 1)
    col = lambda x: jnp.sum(jnp.where(eye, jnp.broadcast_to(x, (kd, kd)), 0.0), axis=1, keepdims=True)
    s = s0_ref[0, 0]
    for t in range(ts):
        s = col(fg[t:t + 1]) * s + col(kk[t:t + 1]) * v[t:t + 1]
        o = jnp.sum(col(q[t:t + 1]) * s, axis=0, keepdims=True)
        o_ref[0, t:t + 1, :] = _hgrn_out(o, g[t:t + 1], gn_ref[...])
    s_ref[0, 0] = s


def _hgrn_sample(zs3, heads, lower_bounds, gnorm, state, layer):
    nb, ts, _ = zs3.shape
    kd = C_KEY_DIM
    depth = lower_bounds.shape[0]
    col = lambda part: pl.BlockSpec((1, ts, kd), lambda b, h: (b, 0, part * heads + h))
    return pl.pallas_call(
        functools.partial(_hgrn_sample_kernel, layer=layer, ts=ts),
        grid=(nb, heads),
        in_specs=[col(0), col(1), col(2), col(3),
                  pl.BlockSpec((depth, kd), lambda b, h: (0, h)),
                  pl.BlockSpec((1, kd), lambda b, h: (0, 0)),
                  pl.BlockSpec((1, 1, kd, kd), lambda b, h: (b, h, 0, 0))],
        out_specs=[pl.BlockSpec((1, ts, kd), lambda b, h: (b, 0, h)),
                   pl.BlockSpec((1, 1, kd, kd), lambda b, h: (b, h, 0, 0))],
        out_shape=[jax.ShapeDtypeStruct((nb, ts, heads * kd), F32),
                   jax.ShapeDtypeStruct((nb, heads, kd, kd), F32)],
        compiler_params=_params(("parallel", "parallel")),
        name="hgrn_sample",
    )(zs3, zs3, zs3, zs3, lower_bounds, gnorm.reshape(1, kd), state)


def _mem_attn_kernel(q_ref, k_ref, v_ref, o_ref, *, batched):
    q = q_ref[0] if batched else q_ref[...]
    k = k_ref[0] if batched else k_ref[...]
    v = v_ref[0] if batched else v_ref[...]
    scale = 1.0 / math.sqrt(q.shape[-1])
    s = lax.dot_general(q.astype(BF16), k.astype(BF16), NT_DIMS, preferred_element_type=F32) * scale
    m = s.max(axis=1, keepdims=True)
    p = jnp.exp(s - m)
    l = p.sum(axis=1, keepdims=True)
    o = jnp.dot(p.astype(BF16), v.astype(BF16), preferred_element_type=F32) / l
    if batched:
        o_ref[0] = o.astype(o_ref.dtype)
    else:
        o_ref[...] = o.astype(o_ref.dtype)


def _mem_attn_prompt(q, mk, mv, nb, t, n_mem):
    w = q.shape[1]
    hd = w // MEM_HEADS
    tq = 512
    nt = t // tq
    return pl.pallas_call(
        functools.partial(_mem_attn_kernel, batched=False),
        grid=(nb, MEM_HEADS, nt),
        in_specs=[pl.BlockSpec((tq, hd), lambda b, h, i: (b * nt + i, h)),
                  pl.BlockSpec((n_mem, hd), lambda b, h, i: (b, h)),
                  pl.BlockSpec((n_mem, hd), lambda b, h, i: (b, h))],
        out_specs=pl.BlockSpec((tq, hd), lambda b, h, i: (b * nt + i, h)),
        out_shape=jax.ShapeDtypeStruct((nb * t, w), BF16),
        compiler_params=_params(("parallel", "parallel", "parallel")),
        name="mem_attn_prompt",
    )(q, mk, mv)


def _mem_attn_sample(q3, mk3, mv3):
    nb, ts, w = q3.shape
    n_mem = mk3.shape[1]
    hd = w // MEM_HEADS
    return pl.pallas_call(
        functools.partial(_mem_attn_kernel, batched=True),
        grid=(nb, MEM_HEADS),
        in_specs=[pl.BlockSpec((1, ts, hd), lambda b, h: (b, 0, h)),
                  pl.BlockSpec((1, n_mem, hd), lambda b, h: (b, 0, h)),
                  pl.BlockSpec((1, n_mem, hd), lambda b, h: (b, 0, h))],
        out_specs=pl.BlockSpec((1, ts, hd), lambda b, h: (b, 0, h)),
        out_shape=jax.ShapeDtypeStruct((nb, ts, w), F32),
        compiler_params=_params(("parallel", "parallel")),
        name="mem_attn_sample",
    )(q3, mk3, mv3)


def _ffn_in_prompt_kernel(x_ref, xp_ref, wg_ref, wu_ref, cw_ref, cb_ref, h_ref, nb_ref, gbuf,
                          *, tm, tiles_per_seq):
    i = pl.program_id(0)
    halo = BF16_SUBLANES
    x = x_ref[...]
    gate = jnp.dot(x, wg_ref[...], preferred_element_type=F32)
    up = jnp.dot(x, wu_ref[...], preferred_element_type=F32)
    gprev = jnp.dot(xp_ref[...], wg_ref[...], preferred_element_type=F32)
    gprev = jnp.where(i % tiles_per_seq == 0, 0.0, gprev)
    gbuf[0:halo, :] = gprev
    gbuf[halo:halo + tm, :] = gate
    cw = cw_ref[...]
    gc = (cw[0:1] * gbuf[halo - 2:halo - 2 + tm, :] + cw[1:2] * gbuf[halo - 1:halo - 1 + tm, :]
          + cw[2:3] * gate + cb_ref[...])
    h_ref[...] = (gc * _sigmoid(gc) * up).astype(h_ref.dtype)
    nb_ref[0] = gate[tm - (FFN_CONV_WIDTH - 1):tm, :]


def _ffn_in_prompt(h, w_in, cw, cb, nb, t, dff):
    m, d = h.shape
    tm, tn = 1024, 256
    halo = BF16_SUBLANES
    tiles_per_seq = t // tm
    nj = dff // tn
    hidden, tails = pl.pallas_call(
        functools.partial(_ffn_in_prompt_kernel, tm=tm, tiles_per_seq=tiles_per_seq),
        grid=(m // tm, nj),
        in_specs=[pl.BlockSpec((tm, d), lambda i, j: (i, 0)),
                  pl.BlockSpec((halo, d), lambda i, j: (jnp.maximum(i * (tm // halo) - 1, 0), 0)),
                  pl.BlockSpec((d, tn), lambda i, j: (0, j)),
                  pl.BlockSpec((d, tn), lambda i, j: (0, nj + j)),
                  pl.BlockSpec((FFN_CONV_WIDTH, tn), lambda i, j: (0, j)),
                  pl.BlockSpec((1, tn), lambda i, j: (0, j))],
        out_specs=[pl.BlockSpec((tm, tn), lambda i, j: (i, j)),
                   pl.BlockSpec((1, FFN_CONV_WIDTH - 1, tn), lambda i, j: (i, 0, j))],
        out_shape=[jax.ShapeDtypeStruct((m, dff), BF16),
                   jax.ShapeDtypeStruct((m // tm, FFN_CONV_WIDTH - 1, dff), F32)],
        scratch_shapes=[pltpu.VMEM((halo + tm, tn), F32)],
        compiler_params=_params(("parallel", "parallel")),
        name="ffn_in_prompt",
    )(h, h, w_in, w_in, cw, cb.reshape(1, dff))
    return hidden, tails[tiles_per_seq - 1::tiles_per_seq]


def _ffn_act_sample_kernel(g_ref, u_ref, cache_ref, cw_ref, cb_ref, h_ref, nb_ref, *, ts):
    w = FFN_CONV_WIDTH
    rows = [cache_ref[0, r:r + 1, :] for r in range(w - 1)] + [g_ref[0, t:t + 1, :] for t in range(ts)]
    for t in range(ts):
        gc = cb_ref[...]
        for j in range(w):
            gc = gc + cw_ref[j:j + 1, :] * rows[t + j]
        h_ref[0, t:t + 1, :] = gc * _sigmoid(gc) * u_ref[0, t:t + 1, :]
    for r in range(w - 1):
        nb_ref[0, r:r + 1, :] = rows[ts + r]


def _ffn_act_sample(zs3, cache, cw, cb, dff):
    nb, ts, _ = zs3.shape
    w = FFN_CONV_WIDTH
    return pl.pallas_call(
        functools.partial(_ffn_act_sample_kernel, ts=ts),
        grid=(nb,),
        in_specs=[pl.BlockSpec((1, ts, dff), lambda b: (b, 0, 0)),
                  pl.BlockSpec((1, ts, dff), lambda b: (b, 0, 1)),
                  pl.BlockSpec((1, w - 1, dff), lambda b: (b, 0, 0)),
                  pl.BlockSpec((w, dff), lambda b: (0, 0)),
                  pl.BlockSpec((1, dff), lambda b: (0, 0))],
        out_specs=[pl.BlockSpec((1, ts, dff), lambda b: (b, 0, 0)),
                   pl.BlockSpec((1, w - 1, dff), lambda b: (b, 0, 0))],
        out_shape=[jax.ShapeDtypeStruct((nb, ts, dff), F32),
                   jax.ShapeDtypeStruct((nb, w - 1, dff), F32)],
        compiler_params=_params(("parallel",)),
        name="ffn_act_sample",
    )(zs3, zs3, cache, cw, cb.reshape(1, dff))


PROMPT_TM = 1024
PROMPT_TN = 512
SAMPLE_TN = 512


def kernel(x_prompt, x_sample, mem_prompt, cache_moba_k, cache_moba_v, page_table, cache_convmod,
           state_hgrn, cache_mem_k, cache_mem_v, cache_ffn_conv, g_mix, g_mem_q, g_mem_kv, g_ffn,
           g_final, w_in_a, w_out_a, conv_w_b, conv_b_b, ln_g_b, ln_b_b, w_in_c, w_out_c,
           lower_bounds, gnorm_c, w_mem_q, w_mem_kv, w_mem_o, w_ffn_in, ffn_conv_w, ffn_conv_b,
           w_ffn_out):
    bp, tp, d = x_prompt.shape
    bs, ts, _ = x_sample.shape
    depth = g_mix.shape[0]
    n_mem = mem_prompt.shape[1]
    n_pages = page_table.shape[1]
    dff = w_ffn_out.shape[1]
    b_ch = conv_w_b.shape[2]
    c_heads = w_out_c.shape[1] // C_KEY_DIM
    mp, ms = bp * tp, bs * ts

    def proj(xp, xs, w, *, res=(None, None), n_off=0, n=None, out_dtype=F32, name="proj"):
        nn = w.shape[1] if n is None else n
        kd = w.shape[0]
        tk = kd if kd <= 4096 else kd // 2
        tn_s = SAMPLE_TN if nn % SAMPLE_TN == 0 else 256
        yp = _mm(xp, w, tm=PROMPT_TM, tn=PROMPT_TN, tk=tk, n_off=n_off, n=nn, res=res[0],
                 out_dtype=out_dtype, name=name + "_prompt")
        ys = _mm(xs, w, tm=ms, tn=tn_s, tk=tk, n_off=n_off, n=nn, res=res[1], name=name + "_sample")
        return yp, ys

    xp = x_prompt.reshape(mp, d)
    xs = x_sample.reshape(ms, d)
    pt_flat = page_table.reshape(-1)
    mem2 = mem_prompt.reshape(bp * n_mem, d)

    outs = {k: [] for k in ("mk_p", "mv_p", "mk_s", "mv_s", "cv_p", "cv_s", "hg_p", "hg_s",
                            "memk", "memv", "ff_p", "ff_s")}
    for l in range(depth):
        j = l // 2
        hp = _rmsnorm(xp, g_mix[l], BF16)
        hs = _rmsnorm(xs, g_mix[l], F32)
        if l % 2 == 0:
            w_in = w_in_a[j].astype(BF16)
            zp, zs = proj(hp, hs, w_in, name="in_a")
            zs3 = zs.reshape(bs, ts, -1)
            zs4 = zs.reshape(bs, ts, 1, -1)
            oa_p = _moba_prompt(zp, bp, tp)
            ck = cache_moba_k[j].reshape(-1, PAGE_SIZE, A_WIDTH)
            cv = cache_moba_v[j].reshape(-1, PAGE_SIZE, A_WIDTH)
            kmean = _moba_kmean(ck, pt_flat, bs, n_pages)
            sel = _moba_select(zs3, kmean.reshape(bs, -1, A_WIDTH), ts)
            oa_s = _moba_sample(zs3, zs4, ck, cv, sel.reshape(-1), pt_flat, bs, ts, n_pages)
            c_p, buf_p = _convmod_prompt(zp, bp, tp, 3 * A_WIDTH, b_ch, conv_w_b[j], conv_b_b[j],
                                         ln_g_b[j], ln_b_b[j])
            c_s, buf_s = _convmod_sample(zs3, cache_convmod[j], 3 * A_WIDTH, b_ch, conv_w_b[j],
                                         conv_b_b[j], ln_g_b[j], ln_b_b[j])
            cat_p = jnp.concatenate([oa_p, c_p], axis=1)
            cat_s = jnp.concatenate([oa_s.reshape(ms, A_WIDTH), c_s.reshape(ms, b_ch)], axis=1)
            xp, xs = proj(cat_p, cat_s, w_out_a[j].astype(BF16), res=(xp, xs), name="out_a")
            outs["mk_p"].append(zp[:, A_WIDTH:2 * A_WIDTH].reshape(bp, tp, A_HEADS, A_HEAD_DIM))
            outs["mv_p"].append(zp[:, 2 * A_WIDTH:3 * A_WIDTH].reshape(bp, tp, A_HEADS, A_HEAD_DIM))
            outs["mk_s"].append(zs[:, A_WIDTH:2 * A_WIDTH].reshape(bs, ts, A_HEADS, A_HEAD_DIM))
            outs["mv_s"].append(zs[:, 2 * A_WIDTH:3 * A_WIDTH].reshape(bs, ts, A_HEADS, A_HEAD_DIM))
            outs["cv_p"].append(buf_p)
            outs["cv_s"].append(buf_s)
        else:
            w_in = w_in_c[j].astype(BF16)
            zp, zs = proj(hp, hs, w_in, name="in_c")
            og_p, st_p = _hgrn_prompt(zp, bp, tp, c_heads, lower_bounds, gnorm_c[j], l)
            og_s, st_s = _hgrn_sample(zs.reshape(bs, ts, -1), c_heads, lower_bounds, gnorm_c[j],
                                      state_hgrn[j], l)
            xp, xs = proj(og_p, og_s.reshape(ms, -1), w_out_c[j].astype(BF16), res=(xp, xs),
                          name="out_c")
            outs["hg_p"].append(st_p)
            outs["hg_s"].append(st_s)

        mw = w_mem_q.shape[2]
        w_kv = w_mem_kv[l].astype(BF16)
        hm = _rmsnorm(mem2, g_mem_kv[l], BF16)
        mk = _mm(hm, w_kv, tm=bp * n_mem, tn=PROMPT_TN, n_off=0, n=mw, name="mem_k")
        mv = _mm(hm, w_kv, tm=bp * n_mem, tn=PROMPT_TN, n_off=mw, n=mw, name="mem_v")
        hp = _rmsnorm(xp, g_mem_q[l], BF16)
        hs = _rmsnorm(xs, g_mem_q[l], F32)
        w_q = w_mem_q[l].astype(BF16)
        qp = _mm(hp, w_q, tm=PROMPT_TM, tn=PROMPT_TN, out_dtype=BF16, name="mem_q_prompt")
        qs = _mm(hs, w_q, tm=ms, tn=SAMPLE_TN, name="mem_q_sample")
        ap = _mem_attn_prompt(qp, mk, mv, bp, tp, n_mem)
        a_s = _mem_attn_sample(qs.reshape(bs, ts, mw), cache_mem_k[l].reshape(bs, n_mem, mw),
                               cache_mem_v[l].reshape(bs, n_mem, mw))
        xp, xs = proj(ap, a_s.reshape(ms, mw), w_mem_o[l].astype(BF16), res=(xp, xs), name="mem_o")
        outs["memk"].append(mk.reshape(bp, n_mem, MEM_HEADS, mw // MEM_HEADS))
        outs["memv"].append(mv.reshape(bp, n_mem, MEM_HEADS, mw // MEM_HEADS))

        hp = _rmsnorm(xp, g_ffn[l], BF16)
        hs = _rmsnorm(xs, g_ffn[l], F32)
        w_fi = w_ffn_in[l].astype(BF16)
        hid_p, fb_p = _ffn_in_prompt(hp, w_fi, ffn_conv_w[l], ffn_conv_b[l], bp, tp, dff)
        zf = _mm(hs, w_fi, tm=ms, tn=SAMPLE_TN, name="ffn_in_sample")
        hid_s, fb_s = _ffn_act_sample(zf.reshape(bs, ts, 2 * dff), cache_ffn_conv[l],
                                      ffn_conv_w[l], ffn_conv_b[l], dff)
        xp, xs = proj(hid_p, hid_s.reshape(ms, dff), w_ffn_out[l].astype(BF16), res=(xp, xs),
                      name="ffn_out")
        outs["ff_p"].append(fb_p)
        outs["ff_s"].append(fb_s)

    y_prompt = _rmsnorm(xp, g_final, F32).reshape(bp, tp, d)
    y_sample = _rmsnorm(xs, g_final, F32).reshape(bs, ts, d)
    st = jnp.stack
    return (y_prompt, y_sample, st(outs["mk_p"]), st(outs["mv_p"]), st(outs["mk_s"]), st(outs["mv_s"]),
            st(outs["cv_p"]), st(outs["cv_s"]), st(outs["hg_p"]), st(outs["hg_s"]),
            st(outs["memk"]), st(outs["memv"]), st(outs["ff_p"]), st(outs["ff_s"]))
```

```python
import functools
import math

import jax
import jax.numpy as jnp
from jax import lax
from jax.experimental import pallas as pl
from jax.experimental.pallas import tpu as pltpu

F32 = jnp.float32
BF16 = jnp.bfloat16

EPS = 1e-6
NEG = -1e30

PAGE_SIZE = 128
A_HEADS = 16
A_HEAD_DIM = 128
A_WIDTH = A_HEADS * A_HEAD_DIM
MOBA_BLOCK = 256
MOBA_TOPK = 3
B_CONV_WIDTH = 31
C_KEY_DIM = 128
MEM_HEADS = 4
FFN_CONV_WIDTH = 3

LANES = 128
BF16_SUBLANES = 16
VMEM_LIMIT = 56 * 1024 * 1024

NT_DIMS = (((1,), (1,)), ((), ()))
TN_DIMS = (((0,), (0,)), ((), ()))


def _params(sem, vmem=VMEM_LIMIT):
    return pltpu.CompilerParams(dimension_semantics=sem, vmem_limit_bytes=vmem)


def _sigmoid(x):
    return 1.0 / (1.0 + jnp.exp(-x))


def _split3(a):
    a1 = a.astype(BF16)
    r1 = a - a1.astype(F32)
    a2 = r1.astype(BF16)
    a3 = (r1 - a2.astype(F32)).astype(BF16)
    return a1, a2, a3


def _dot_f32(a, b, dims):
    a1, a2, a3 = _split3(a)
    b1, b2, b3 = _split3(b)
    d = lambda x, y: lax.dot_general(x, y, dims, preferred_element_type=F32)
    low = d(a1, b3) + d(a3, b1) + d(a2, b2)
    mid = d(a1, b2) + d(a2, b1)
    return (low + mid) + d(a1, b1)


def _rmsnorm_kernel(x_ref, g_ref, o_ref):
    x = x_ref[...]
    ms = jnp.mean(x * x, axis=-1, keepdims=True)
    o_ref[...] = (x * lax.rsqrt(ms + EPS) * g_ref[...]).astype(o_ref.dtype)


def _rmsnorm(x, g, out_dtype):
    m, d = x.shape
    tm = min(m, 512)
    return pl.pallas_call(
        _rmsnorm_kernel,
        grid=(m // tm,),
        in_specs=[pl.BlockSpec((tm, d), lambda i: (i, 0)),
                  pl.BlockSpec((1, d), lambda i: (0, 0))],
        out_specs=pl.BlockSpec((tm, d), lambda i: (i, 0)),
        out_shape=jax.ShapeDtypeStruct((m, d), out_dtype),
        compiler_params=_params(("parallel",)),
        name="rmsnorm",
    )(x, g.reshape(1, d))


PROMPT_TM = 1024
PROJ_TN = 512


def _proj_kernel(*refs, has_res):
    if has_res:
        xp_ref, xs_ref, w_ref, rp_ref, rs_ref, yp_ref, ys_ref, wb = refs
    else:
        xp_ref, xs_ref, w_ref, yp_ref, ys_ref, wb = refs

    @pl.when(pl.program_id(1) == 0)
    def _():
        wb[...] = w_ref[0].astype(BF16)
        ys = jnp.dot(xs_ref[...].astype(BF16), wb[...], preferred_element_type=F32)
        if has_res:
            ys = ys + rs_ref[...]
        ys_ref[...] = ys.astype(ys_ref.dtype)

    acc = jnp.dot(xp_ref[...], wb[...], preferred_element_type=F32)
    if has_res:
        acc = acc + rp_ref[...]
    yp_ref[...] = acc.astype(yp_ref.dtype)


def _proj(xp, xs, w, layer, *, n_off=0, n=None, res=None, out_dtype=F32, name="proj"):
    mp, kdim = xp.shape
    ms = xs.shape[0]
    n = w.shape[2] if n is None else n
    tm, tn = PROMPT_TM, PROJ_TN
    assert mp % tm == 0 and n % tn == 0 and n_off % tn == 0 and w.shape[1] == kdim
    joff = n_off // tn
    in_specs = [pl.BlockSpec((tm, kdim), lambda j, i: (i, 0)),
                pl.BlockSpec((ms, kdim), lambda j, i: (0, 0)),
                pl.BlockSpec((1, kdim, tn), lambda j, i: (layer, 0, j + joff))]
    args = [xp, xs, w]
    if res is not None:
        in_specs += [pl.BlockSpec((tm, tn), lambda j, i: (i, j)),
                     pl.BlockSpec((ms, tn), lambda j, i: (0, j))]
        args += list(res)
    return pl.pallas_call(
        functools.partial(_proj_kernel, has_res=res is not None),
        grid=(n // tn, mp // tm),
        in_specs=in_specs,
        out_specs=[pl.BlockSpec((tm, tn), lambda j, i: (i, j)),
                   pl.BlockSpec((ms, tn), lambda j, i: (0, j))],
        out_shape=[jax.ShapeDtypeStruct((mp, n), out_dtype),
                   jax.ShapeDtypeStruct((ms, n), F32)],
        scratch_shapes=[pltpu.VMEM((kdim, tn), BF16)],
        compiler_params=_params(("parallel", "arbitrary")),
        name=name,
    )(*args)


def _mm_kernel(x_ref, w_ref, *rest, nk, has_res):
    if has_res:
        r_ref, o_ref = rest
    else:
        (o_ref,) = rest
    acc = jnp.dot(x_ref[...].astype(BF16), w_ref[0].astype(BF16), preferred_element_type=F32)
    if nk == 1:
        if has_res:
            acc = acc + r_ref[...]
        o_ref[...] = acc.astype(o_ref.dtype)
    else:
        k = pl.program_id(2)

        @pl.when(k == 0)
        def _():
            o_ref[...] = (acc + r_ref[...]) if has_res else acc

        @pl.when(k > 0)
        def _():
            o_ref[...] += acc


def _mm(x, w, layer, *, tm, tn, tk=None, n_off=0, n=None, res=None, out_dtype=F32, name="matmul"):
    m, kdim = x.shape
    n = w.shape[2] if n is None else n
    tk = kdim if tk is None else tk
    nk = kdim // tk
    assert m % tm == 0 and n % tn == 0 and kdim % tk == 0 and n_off % tn == 0
    assert nk == 1 or out_dtype == F32
    joff = n_off // tn
    in_specs = [pl.BlockSpec((tm, tk), lambda i, j, k: (i, k)),
                pl.BlockSpec((1, tk, tn), lambda i, j, k: (layer, k, j + joff))]
    args = [x, w]
    if res is not None:
        in_specs.append(pl.BlockSpec((tm, tn), lambda i, j, k: (i, j)))
        args.append(res)
    return pl.pallas_call(
        functools.partial(_mm_kernel, nk=nk, has_res=res is not None),
        grid=(m // tm, n // tn, nk),
        in_specs=in_specs,
        out_specs=pl.BlockSpec((tm, tn), lambda i, j, k: (i, j)),
        out_shape=jax.ShapeDtypeStruct((m, n), out_dtype),
        compiler_params=_params(("parallel", "parallel", "arbitrary")),
        name=name,
    )(*args)


def _moba_prompt_kernel(q_ref, k_ref, v_ref, o_ref, *, nblk):
    blk = MOBA_BLOCK
    scale = 1.0 / math.sqrt(A_HEAD_DIM)
    k = k_ref[...]
    kb = k.astype(BF16)
    vb = v_ref[...].astype(BF16)
    kmean = jnp.mean(k.reshape(nblk, blk, A_HEAD_DIM), axis=1)
    krow = lax.broadcasted_iota(jnp.int32, (blk, blk), 0)
    qcol = lax.broadcasted_iota(jnp.int32, (blk, blk), 1)
    causal = krow <= qcol
    for qb in range(nblk):
        qf = q_ref[qb * blk:(qb + 1) * blk, :]
        qbf = (qf * scale).astype(BF16)
        sel = None
        if qb > MOBA_TOPK:
            gate = _dot_f32(kmean, qf, NT_DIMS)
            rows = [gate[n:n + 1, :] for n in range(qb)]
            sel = []
            for n in range(qb):
                rank = jnp.zeros((1, blk), jnp.int32)
                for n2 in range(qb):
                    if n2 == n:
                        continue
                    beats = (rows[n2] >= rows[n]) if n2 < n else (rows[n2] > rows[n])
                    rank = rank + beats.astype(jnp.int32)
                sel.append(rank < MOBA_TOPK)
        s_list = []
        for n in range(qb + 1):
            s = lax.dot_general(kb[n * blk:(n + 1) * blk], qbf, NT_DIMS,
                                preferred_element_type=F32)
            if n == qb:
                s = jnp.where(causal, s, NEG)
            elif sel is not None:
                s = jnp.where(sel[n], s, NEG)
            s_list.append(s)
        m = s_list[0].max(axis=0, keepdims=True)
        for s in s_list[1:]:
            m = jnp.maximum(m, s.max(axis=0, keepdims=True))
        l = jnp.zeros((1, blk), F32)
        ot = jnp.zeros((A_HEAD_DIM, blk), F32)
        for n, s in enumerate(s_list):
            p = jnp.exp(s - m)
            l = l + p.sum(axis=0, keepdims=True)
            ot = ot + lax.dot_general(vb[n * blk:(n + 1) * blk], p.astype(BF16), TN_DIMS,
                                      preferred_element_type=F32)
        o_ref[qb * blk:(qb + 1) * blk, :] = (ot / l).T.astype(o_ref.dtype)


def _moba_prompt(q, k, v, nb, t, out_width):
    hd = A_HEAD_DIM
    spec = lambda: pl.BlockSpec((t, hd), lambda b, h: (b, h))
    return pl.pallas_call(
        functools.partial(_moba_prompt_kernel, nblk=t // MOBA_BLOCK),
        grid=(nb, A_HEADS),
        in_specs=[spec(), spec(), spec()],
        out_specs=spec(),
        out_shape=jax.ShapeDtypeStruct((nb * t, out_width), BF16),
        compiler_params=_params(("parallel", "parallel")),
        name="moba_prompt",
    )(q, k, v)


PAGES_PER_BLOCK = MOBA_BLOCK // PAGE_SIZE
KMEAN_BLOCKS = 2


def _kmean_kernel(pt_ref, *refs):
    o_ref = refs[-1]
    for i in range(KMEAN_BLOCKS):
        s = jnp.zeros((1, A_WIDTH), F32)
        for pg in range(PAGES_PER_BLOCK):
            s = s + jnp.sum(refs[i * PAGES_PER_BLOCK + pg][0], axis=0, keepdims=True)
        o_ref[0, i] = s * (1.0 / MOBA_BLOCK)


def _moba_kmean(cache_k, pt_flat, page0, nb, n_pages):
    nblk = n_pages // PAGES_PER_BLOCK
    per_step = KMEAN_BLOCKS * PAGES_PER_BLOCK
    page = lambda pg: pl.BlockSpec(
        (1, PAGE_SIZE, A_WIDTH), lambda b, n, pt: (page0 + pt[b * n_pages + per_step * n + pg], 0, 0))
    return pl.pallas_call(
        _kmean_kernel,
        grid_spec=pltpu.PrefetchScalarGridSpec(
            num_scalar_prefetch=1,
            grid=(nb, nblk // KMEAN_BLOCKS),
            in_specs=[page(pg) for pg in range(per_step)],
            out_specs=pl.BlockSpec((1, KMEAN_BLOCKS, 1, A_WIDTH), lambda b, n, pt: (b, n, 0, 0)),
        ),
        out_shape=jax.ShapeDtypeStruct((nb, nblk, 1, A_WIDTH), F32),
        compiler_params=_params(("parallel", "parallel")),
        name="moba_kmean",
    )(pt_flat, *([cache_k] * per_step))


def _moba_select_kernel(q_ref, km_ref, o_ref, *, ts, nblk):
    q = q_ref[0]
    km = km_ref[0]
    nq = ts * A_HEADS
    qrep = jnp.concatenate(
        [jnp.broadcast_to(q[t:t + 1], (A_HEADS, A_WIDTH)) for t in range(ts)], axis=0)
    rowh = lax.broadcasted_iota(jnp.int32, (nq, A_WIDTH), 0) % A_HEADS
    colh = lax.broadcasted_iota(jnp.int32, (nq, A_WIDTH), 1) // A_HEAD_DIM
    qbig = jnp.where(rowh == colh, qrep, 0.0)
    gate = _dot_f32(km, qbig, NT_DIMS)
    nidx = lax.broadcasted_iota(jnp.int32, (nblk, nq), 0)
    for r in range(MOBA_TOPK):
        m = gate.max(axis=0, keepdims=True)
        idx = jnp.min(jnp.where(gate == m, nidx, nblk), axis=0, keepdims=True)
        o_ref[0, r:r + 1, :] = idx
        gate = jnp.where(nidx == idx, -jnp.inf, gate)


def _moba_select(qs3, kmean):
    nb, ts, _ = qs3.shape
    nblk = kmean.shape[1]
    nq = ts * A_HEADS
    return pl.pallas_call(
        functools.partial(_moba_select_kernel, ts=ts, nblk=nblk),
        grid=(nb,),
        in_specs=[pl.BlockSpec((1, ts, A_WIDTH), lambda b: (b, 0, 0)),
                  pl.BlockSpec((1, nblk, A_WIDTH), lambda b: (b, 0, 0))],
        out_specs=pl.BlockSpec((1, MOBA_TOPK, nq), lambda b: (b, 0, 0)),
        out_shape=jax.ShapeDtypeStruct((nb, MOBA_TOPK, nq), jnp.int32),
        compiler_params=_params(("parallel",)),
        name="moba_select",
    )(qs3, kmean)


def _moba_sample_kernel(sel_ref, pt_ref, q_ref, kn_ref, vn_ref, *rest, ts, npg):
    k_refs = rest[:ts * npg]
    v_refs = rest[ts * npg:2 * ts * npg]
    o_ref = rest[2 * ts * npg]
    scale = 1.0 / math.sqrt(A_HEAD_DIM)
    trow = lax.broadcasted_iota(jnp.int32, (ts, 1), 0)
    for t in range(ts):
        q = q_ref[0, t:t + 1, :] * scale
        kts = k_refs[t * npg:(t + 1) * npg]
        vts = v_refs[t * npg:(t + 1) * npg]
        s_cols = [jnp.sum(kr[0] * q, axis=1, keepdims=True) for kr in kts]
        s_own = jnp.sum(kn_ref[0] * q, axis=1, keepdims=True)
        s_own = jnp.where(trow <= t, s_own, NEG)
        m = s_own.max(axis=0, keepdims=True)
        for s in s_cols:
            m = jnp.maximum(m, s.max(axis=0, keepdims=True))
        p_own = jnp.exp(s_own - m)
        l = p_own.sum(axis=0, keepdims=True)
        o = jnp.sum(p_own * vn_ref[0], axis=0, keepdims=True)
        for s, vr in zip(s_cols, vts):
            p = jnp.exp(s - m)
            l = l + p.sum(axis=0, keepdims=True)
            o = o + jnp.sum(p * vr[0], axis=0, keepdims=True)
        o_ref[0, t:t + 1, :] = o / l


def _moba_sample(qs3, ks3, vs3, cache_k, cache_v, sel_flat, pt_flat, page0, n_pages):
    nb, ts, _ = qs3.shape
    hd = A_HEAD_DIM
    nblk = n_pages // PAGES_PER_BLOCK
    nq = ts * A_HEADS
    npg = MOBA_TOPK * PAGES_PER_BLOCK

    def page_spec(t, r, pg):
        def imap(b, h, sel, pt):
            blkid = jnp.minimum(sel[b * (MOBA_TOPK * nq) + r * nq + t * A_HEADS + h], nblk - 1)
            return (page0 + pt[b * n_pages + blkid * PAGES_PER_BLOCK + pg], 0, h)
        return pl.BlockSpec((1, PAGE_SIZE, hd), imap)

    pages = [page_spec(t, r, pg) for t in range(ts) for r in range(MOBA_TOPK)
             for pg in range(PAGES_PER_BLOCK)]
    own = lambda: pl.BlockSpec((1, ts, hd), lambda b, h, sel, pt: (b, 0, h))
    return pl.pallas_call(
        functools.partial(_moba_sample_kernel, ts=ts, npg=npg),
        grid_spec=pltpu.PrefetchScalarGridSpec(
            num_scalar_prefetch=2,
            grid=(nb, A_HEADS),
            in_specs=[own(), own(), own()] + pages + pages,
            out_specs=own(),
        ),
        out_shape=jax.ShapeDtypeStruct((nb, ts, A_WIDTH), F32),
        compiler_params=_params(("parallel", "parallel")),
        name="moba_sample",
    )(sel_flat, pt_flat, qs3, ks3, vs3, *([cache_k] * len(pages)), *([cache_v] * len(pages)))


def _layernorm_swish(y, g, b):
    mu = jnp.mean(y, axis=-1, keepdims=True)
    yc = y - mu
    var = jnp.mean(yc * yc, axis=-1, keepdims=True)
    yn = yc * lax.rsqrt(var + EPS) * g + b
    return yn * _sigmoid(yn)


CONV_HALO = 32
CONV_ROWS = 32
CONV_COLS = 256


def _convmod_prompt_kernel(ga_ref, gb_ref, cw_ref, cb_ref, lg_ref, lb_ref, cat_ref, c_ref, nb_ref,
                           ubuf, ybuf, *, tt, nt):
    del cat_ref
    w = B_CONV_WIDTH
    ch = ubuf.shape[1]
    t = pl.program_id(1)

    @pl.when(t == 0)
    def _():
        ubuf[0:CONV_HALO, :] = jnp.zeros((CONV_HALO, ch), F32)

    ubuf[CONV_HALO:CONV_HALO + tt, :] = ga_ref[...] * _sigmoid(gb_ref[...])
    base = CONV_HALO - (w - 1)
    for c0 in range(0, ch, CONV_COLS):
        for r0 in range(0, tt, CONV_ROWS):
            acc = jnp.broadcast_to(cb_ref[:, c0:c0 + CONV_COLS], (CONV_ROWS, CONV_COLS))
            for j in range(w):
                acc = acc + cw_ref[j:j + 1, c0:c0 + CONV_COLS] * \
                    ubuf[r0 + j + base:r0 + j + base + CONV_ROWS, c0:c0 + CONV_COLS]
            ybuf[r0:r0 + CONV_ROWS, c0:c0 + CONV_COLS] = acc
    c_ref[...] = _layernorm_swish(ybuf[...], lg_ref[...], lb_ref[...]).astype(c_ref.dtype)

    @pl.when(t == nt - 1)
    def _():
        nb_ref[0] = ubuf[CONV_HALO + tt - (w - 1):CONV_HALO + tt, :]

    ubuf[0:CONV_HALO, :] = ubuf[tt:tt + CONV_HALO, :]


def _convmod_prompt(glu, cat, nb, t, cw, cb, lg, lb):
    ch = glu.shape[1] // 2
    tt = 256
    nt = t // tt
    vec = lambda: pl.BlockSpec((1, ch), lambda b, i: (0, 0))
    return pl.pallas_call(
        functools.partial(_convmod_prompt_kernel, tt=tt, nt=nt),
        grid=(nb, nt),
        in_specs=[pl.BlockSpec((tt, ch), lambda b, i: (b * nt + i, 0)),
                  pl.BlockSpec((tt, ch), lambda b, i: (b * nt + i, 1)),
                  pl.BlockSpec((B_CONV_WIDTH, ch), lambda b, i: (0, 0)),
                  vec(), vec(), vec(),
                  pl.BlockSpec(memory_space=pl.ANY)],
        out_specs=[pl.BlockSpec((tt, ch), lambda b, i: (b * nt + i, 1)),
                   pl.BlockSpec((1, B_CONV_WIDTH - 1, ch), lambda b, i: (b, 0, 0))],
        out_shape=[jax.ShapeDtypeStruct(cat.shape, cat.dtype),
                   jax.ShapeDtypeStruct((nb, B_CONV_WIDTH - 1, ch), F32)],
        scratch_shapes=[pltpu.VMEM((CONV_HALO + tt, ch), F32), pltpu.VMEM((tt, ch), F32)],
        input_output_aliases={6: 0},
        compiler_params=_params(("parallel", "arbitrary")),
        name="convmod_prompt",
    )(glu, glu, cw, cb.reshape(1, ch), lg.reshape(1, ch), lb.reshape(1, ch), cat)


def _convmod_sample_kernel(ga_ref, gb_ref, cache_ref, cw_ref, cb_ref, lg_ref, lb_ref,
                           c_ref, nb_ref, ubuf, *, ts):
    w = B_CONV_WIDTH
    ch = ubuf.shape[1]
    ubuf[0:w - 1, :] = cache_ref[0]
    ubuf[w - 1:w - 1 + ts, :] = ga_ref[0] * _sigmoid(gb_ref[0])
    acc = jnp.broadcast_to(cb_ref[...], (ts, ch))
    for j in range(w):
        acc = acc + cw_ref[j:j + 1, :] * ubuf[j:j + ts, :]
    c_ref[0] = _layernorm_swish(acc, lg_ref[...], lb_ref[...])
    nb_ref[0] = ubuf[ts:ts + w - 1, :]


def _convmod_sample(glu3, cache, cw, cb, lg, lb):
    nb, ts, ch2 = glu3.shape
    ch = ch2 // 2
    w = B_CONV_WIDTH
    vec = lambda: pl.BlockSpec((1, ch), lambda b: (0, 0))
    return pl.pallas_call(
        functools.partial(_convmod_sample_kernel, ts=ts),
        grid=(nb,),
        in_specs=[pl.BlockSpec((1, ts, ch), lambda b: (b, 0, 0)),
                  pl.BlockSpec((1, ts, ch), lambda b: (b, 0, 1)),
                  pl.BlockSpec((1, w - 1, ch), lambda b: (b, 0, 0)),
                  pl.BlockSpec((w, ch), lambda b: (0, 0)),
                  vec(), vec(), vec()],
        out_specs=[pl.BlockSpec((1, ts, ch), lambda b: (b, 0, 0)),
                   pl.BlockSpec((1, w - 1, ch), lambda b: (b, 0, 0))],
        out_shape=[jax.ShapeDtypeStruct((nb, ts, ch), F32),
                   jax.ShapeDtypeStruct((nb, w - 1, ch), F32)],
        scratch_shapes=[pltpu.VMEM((w - 1 + ts + 5, ch), F32)],
        compiler_params=_params(("parallel",)),
        name="convmod_sample",
    )(glu3, glu3, cache, cw, cb.reshape(1, ch), lg.reshape(1, ch), lb.reshape(1, ch))


HG_CHUNK = 128
HG_SUB = 16
HG_HEADS = 4


def _hgrn_lower_bound(lbp, layer):
    e = jnp.exp(lbp - lbp.max(axis=0, keepdims=True))
    sm = e / e.sum(axis=0, keepdims=True)
    lb = jnp.zeros_like(sm[0:1])
    for i in range(1, layer + 1):
        lb = lb + sm[i:i + 1]
    return lb


def _hgrn_out(o, g, gn):
    ms = jnp.mean(o * o, axis=-1, keepdims=True)
    return (o * lax.rsqrt(ms + EPS) * gn) * (g * _sigmoid(g))


def _hgrn_chunk(q, f, v, g, lb, gn, s_prev, kpad, bpad, consts):
    cs, sub, kd = HG_CHUNK, HG_SUB, C_KEY_DIM
    tril, rowk, eye, lane, blockdiag = consts
    fg = lb + (1.0 - lb) * _sigmoid(f)
    kk = 1.0 - fg
    vb = v.astype(BF16)
    l1, l2, l3 = _split3(jnp.log(fg))
    cum = lambda x: jnp.dot(tril, x, preferred_element_type=F32)
    b = (cum(l3) + cum(l2)) + cum(l1)

    o = jnp.dot((q * jnp.exp(b)).astype(BF16), s_prev.astype(BF16), preferred_element_type=F32)

    bref_rows = jnp.concatenate(
        [jnp.zeros((sub, kd), F32)]
        + [jnp.broadcast_to(b[sub * i - 1:sub * i], (sub, kd)) for i in range(1, cs // sub)], axis=0)
    qe = (q * jnp.exp(b - bref_rows)).astype(BF16)
    parts = [jnp.zeros((sub, cs), F32)]
    for i in range(1, cs // sub):
        bref = b[sub * i - 1:sub * i]
        ke = jnp.where(rowk < sub * i, kk * jnp.exp(jnp.minimum(bref - b, 0.0)), 0.0).astype(BF16)
        parts.append(lax.dot_general(qe[sub * i:sub * (i + 1)], ke, NT_DIMS,
                                     preferred_element_type=F32))
    a = jnp.concatenate(parts, axis=0)

    kpad[sub:, :] = kk
    bpad[sub:, :] = b
    band = jnp.zeros((cs, cs), F32)
    for d in range(sub):
        if d == 0:
            p = q * kk
        else:
            p = q * kpad[sub - d:sub - d + cs, :] * jnp.exp(b - bpad[sub - d:sub - d + cs, :])
        band = jnp.where(lane == (cs - d) % cs, p.sum(axis=1, keepdims=True), band)
    band = pltpu.roll(band, 0, 1, stride=1, stride_axis=0)
    a = a + jnp.where(blockdiag, band, 0.0)
    o = o + jnp.dot(a.astype(BF16), vb, preferred_element_type=F32)

    bl = b[cs - 1:cs]
    kd_mat = (kk * jnp.exp(bl - b)).astype(BF16)
    decay_col = jnp.sum(jnp.where(eye, jnp.broadcast_to(jnp.exp(bl), (kd, kd)), 0.0),
                        axis=1, keepdims=True)
    s_new = decay_col * s_prev + lax.dot_general(kd_mat, vb, TN_DIMS, preferred_element_type=F32)
    return _hgrn_out(o, g, gn), s_new


def _hgrn_prompt_kernel(q_ref, f_ref, i_ref, g_ref, lbp_ref, gn_ref, o_ref, s_ref,
                        kpad, bpad, *, layer):
    cs, sub, kd = HG_CHUNK, HG_SUB, C_KEY_DIM

    @pl.when(pl.program_id(2) == 0)
    def _():
        s_ref[...] = jnp.zeros_like(s_ref)

    rr = lax.broadcasted_iota(jnp.int32, (cs, cs), 0)
    cc = lax.broadcasted_iota(jnp.int32, (cs, cs), 1)
    consts = (jnp.where(cc <= rr, 1.0, 0.0).astype(BF16),
              lax.broadcasted_iota(jnp.int32, (cs, kd), 0),
              rr == cc,
              cc,
              (rr // sub == cc // sub) & (cc <= rr))
    kpad[:, 0:sub, :] = jnp.zeros((HG_HEADS, sub, kd), F32)
    bpad[:, 0:sub, :] = jnp.zeros((HG_HEADS, sub, kd), F32)
    lbs = _hgrn_lower_bound(lbp_ref[...], layer)
    for h in range(HG_HEADS):
        cols = slice(h * kd, (h + 1) * kd)
        og, s_new = _hgrn_chunk(q_ref[:, cols], f_ref[:, cols], i_ref[:, cols], g_ref[:, cols],
                                lbs[:, cols], gn_ref[...], s_ref[0, h],
                                kpad.at[h], bpad.at[h], consts)
        s_ref[0, h] = s_new
        o_ref[:, cols] = og.astype(o_ref.dtype)


def _hgrn_prompt(z, nb, t, heads, lower_bounds, gnorm, layer):
    kd = C_KEY_DIM
    assert HG_CHUNK == kd and heads % HG_HEADS == 0
    nchunk = t // HG_CHUNK
    depth = lower_bounds.shape[0]
    hw = HG_HEADS * kd
    ng = heads // HG_HEADS
    col = lambda part: pl.BlockSpec((HG_CHUNK, hw), lambda b, h, c: (b * nchunk + c, part * ng + h))
    return pl.pallas_call(
        functools.partial(_hgrn_prompt_kernel, layer=layer),
        grid=(nb, ng, nchunk),
        in_specs=[col(0), col(1), col(2), col(3),
                  pl.BlockSpec((depth, hw), lambda b, h, c: (0, h)),
                  pl.BlockSpec((1, kd), lambda b, h, c: (0, 0))],
        out_specs=[pl.BlockSpec((HG_CHUNK, hw), lambda b, h, c: (b * nchunk + c, h)),
                   pl.BlockSpec((1, HG_HEADS, kd, kd), lambda b, h, c: (b, h, 0, 0))],
        out_shape=[jax.ShapeDtypeStruct((nb * t, heads * kd), BF16),
                   jax.ShapeDtypeStruct((nb, heads, kd, kd), F32)],
        scratch_shapes=[pltpu.VMEM((HG_HEADS, HG_SUB + HG_CHUNK, kd), F32),
                        pltpu.VMEM((HG_HEADS, HG_SUB + HG_CHUNK, kd), F32)],
        compiler_params=_params(("parallel", "parallel", "arbitrary")),
        name="hgrn_prompt",
    )(z, z, z, z, lower_bounds, gnorm.reshape(1, kd))


def _hgrn_sample_kernel(q_ref, f_ref, i_ref, g_ref, lbp_ref, gn_ref, s0_ref, o_ref, s_ref,
                        *, layer, ts):
    kd = C_KEY_DIM
    lb = _hgrn_lower_bound(lbp_ref[...], layer)
    fg = lb + (1.0 - lb) * _sigmoid(f_ref[0])
    kk = 1.0 - fg
    q = q_ref[0]
    v = i_ref[0]
    g = g_ref[0]
    eye = lax.broadcasted_iota(jnp.int32, (kd, kd), 0) == lax.broadcasted_iota(jnp.int32, (kd, kd), 1)
    col = lambda x: jnp.sum(jnp.where(eye, jnp.broadcast_to(x, (kd, kd)), 0.0), axis=1, keepdims=True)
    s = s0_ref[0, 0]
    for t in range(ts):
        s = col(fg[t:t + 1]) * s + col(kk[t:t + 1]) * v[t:t + 1]
        o = jnp.sum(col(q[t:t + 1]) * s, axis=0, keepdims=True)
        o_ref[0, t:t + 1, :] = _hgrn_out(o, g[t:t + 1], gn_ref[...])
    s_ref[0, 0] = s


def _hgrn_sample(zs3, heads, lower_bounds, gnorm, state, j):
    nb, ts, _ = zs3.shape
    kd = C_KEY_DIM
    depth = lower_bounds.shape[0]
    st = state.reshape(-1, heads, kd, kd)
    col = lambda part: pl.BlockSpec((1, ts, kd), lambda b, h: (b, 0, part * heads + h))
    return pl.pallas_call(
        functools.partial(_hgrn_sample_kernel, layer=2 * j + 1, ts=ts),
        grid=(nb, heads),
        in_specs=[col(0), col(1), col(2), col(3),
                  pl.BlockSpec((depth, kd), lambda b, h: (0, h)),
                  pl.BlockSpec((1, kd), lambda b, h: (0, 0)),
                  pl.BlockSpec((1, 1, kd, kd), lambda b, h: (j * nb + b, h, 0, 0))],
        out_specs=[pl.BlockSpec((1, ts, kd), lambda b, h: (b, 0, h)),
                   pl.BlockSpec((1, 1, kd, kd), lambda b, h: (b, h, 0, 0))],
        out_shape=[jax.ShapeDtypeStruct((nb, ts, heads * kd), F32),
                   jax.ShapeDtypeStruct((nb, heads, kd, kd), F32)],
        compiler_params=_params(("parallel", "parallel")),
        name="hgrn_sample",
    )(zs3, zs3, zs3, zs3, lower_bounds, gnorm.reshape(1, kd), st)


def _softmax_attend(q, k, v):
    scale = 1.0 / math.sqrt(q.shape[-1])
    s = lax.dot_general(q.astype(BF16), k.astype(BF16), NT_DIMS, preferred_element_type=F32) * scale
    m = s.max(axis=1, keepdims=True)
    p = jnp.exp(s - m)
    l = p.sum(axis=1, keepdims=True)
    return jnp.dot(p.astype(BF16), v.astype(BF16), preferred_element_type=F32) / l


def _mem_attn_prompt_kernel(q_ref, k_ref, v_ref, o_ref):
    o_ref[...] = _softmax_attend(q_ref[...], k_ref[...], v_ref[...]).astype(o_ref.dtype)


def _mem_attn_prompt(q, mk, mv, nb, t, n_mem):
    w = q.shape[1]
    hd = w // MEM_HEADS
    tq = 512
    nt = t // tq
    return pl.pallas_call(
        _mem_attn_prompt_kernel,
        grid=(nb, MEM_HEADS, nt),
        in_specs=[pl.BlockSpec((tq, hd), lambda b, h, i: (b * nt + i, h)),
                  pl.BlockSpec((n_mem, hd), lambda b, h, i: (b, h)),
                  pl.BlockSpec((n_mem, hd), lambda b, h, i: (b, h))],
        out_specs=pl.BlockSpec((tq, hd), lambda b, h, i: (b * nt + i, h)),
        out_shape=jax.ShapeDtypeStruct((nb * t, w), BF16),
        compiler_params=_params(("parallel", "parallel", "parallel")),
        name="mem_attn_prompt",
    )(q, mk, mv)


def _mem_attn_sample_kernel(q_ref, k_ref, v_ref, o_ref):
    o_ref[0] = _softmax_attend(q_ref[0], k_ref[0, 0], v_ref[0, 0])


def _mem_attn_sample(q3, cache_k, cache_v, layer):
    nb, ts, w = q3.shape
    n_mem = cache_k.shape[2]
    hd = w // MEM_HEADS
    kv = lambda: pl.BlockSpec((1, 1, n_mem, hd), lambda b, h: (layer, b, 0, h))
    return pl.pallas_call(
        _mem_attn_sample_kernel,
        grid=(nb, MEM_HEADS),
        in_specs=[pl.BlockSpec((1, ts, hd), lambda b, h: (b, 0, h)), kv(), kv()],
        out_specs=pl.BlockSpec((1, ts, hd), lambda b, h: (b, 0, h)),
        out_shape=jax.ShapeDtypeStruct((nb, ts, w), F32),
        compiler_params=_params(("parallel", "parallel")),
        name="mem_attn_sample",
    )(q3, cache_k, cache_v)


FFN_TN = 256


def _ffn_in_kernel(xp_ref, xprev_ref, xs_ref, wg_ref, wu_ref, cw_ref, cb_ref,
                   h_ref, tail_ref, gs_ref, us_ref, wgb, wub, gbuf, *, tm, tiles_per_seq):
    i = pl.program_id(1)
    halo = BF16_SUBLANES

    @pl.when(i == 0)
    def _():
        wgb[...] = wg_ref[0].astype(BF16)
        wub[...] = wu_ref[0].astype(BF16)
        xs = xs_ref[...].astype(BF16)
        gs_ref[...] = jnp.dot(xs, wgb[...], preferred_element_type=F32)
        us_ref[...] = jnp.dot(xs, wub[...], preferred_element_type=F32)

    x = xp_ref[...]
    gate = jnp.dot(x, wgb[...], preferred_element_type=F32)
    up = jnp.dot(x, wub[...], preferred_element_type=F32)
    gprev = jnp.dot(xprev_ref[...], wgb[...], preferred_element_type=F32)
    gprev = jnp.where(i % tiles_per_seq == 0, 0.0, gprev)
    gbuf[0:halo, :] = gprev
    gbuf[halo:halo + tm, :] = gate
    cw = cw_ref[...]
    gc = (cw[0:1] * gbuf[halo - 2:halo - 2 + tm, :] + cw[1:2] * gbuf[halo - 1:halo - 1 + tm, :]
          + cw[2:3] * gate + cb_ref[...])
    h_ref[...] = (gc * _sigmoid(gc) * up).astype(h_ref.dtype)
    tail_ref[0] = gate[tm - (FFN_CONV_WIDTH - 1):tm, :]


def _ffn_in(hp, hs, w_in, layer, cw, cb, t, dff):
    m, d = hp.shape
    ms = hs.shape[0]
    tm, tn = PROMPT_TM, FFN_TN
    halo = BF16_SUBLANES
    tiles_per_seq = t // tm
    nj = dff // tn
    hidden, tails, gs, us = pl.pallas_call(
        functools.partial(_ffn_in_kernel, tm=tm, tiles_per_seq=tiles_per_seq),
        grid=(nj, m // tm),
        in_specs=[pl.BlockSpec((tm, d), lambda j, i: (i, 0)),
                  pl.BlockSpec((halo, d), lambda j, i: (jnp.maximum(i * (tm // halo) - 1, 0), 0)),
                  pl.BlockSpec((ms, d), lambda j, i: (0, 0)),
                  pl.BlockSpec((1, d, tn), lambda j, i: (layer, 0, j)),
                  pl.BlockSpec((1, d, tn), lambda j, i: (layer, 0, nj + j)),
                  pl.BlockSpec((FFN_CONV_WIDTH, tn), lambda j, i: (0, j)),
                  pl.BlockSpec((1, tn), lambda j, i: (0, j))],
        out_specs=[pl.BlockSpec((tm, tn), lambda j, i: (i, j)),
                   pl.BlockSpec((1, FFN_CONV_WIDTH - 1, tn), lambda j, i: (i, 0, j)),
                   pl.BlockSpec((ms, tn), lambda j, i: (0, j)),
                   pl.BlockSpec((ms, tn), lambda j, i: (0, j))],
        out_shape=[jax.ShapeDtypeStruct((m, dff), BF16),
                   jax.ShapeDtypeStruct((m // tm, FFN_CONV_WIDTH - 1, dff), F32),
                   jax.ShapeDtypeStruct((ms, dff), F32),
                   jax.ShapeDtypeStruct((ms, dff), F32)],
        scratch_shapes=[pltpu.VMEM((d, tn), BF16), pltpu.VMEM((d, tn), BF16),
                        pltpu.VMEM((halo + tm, tn), F32)],
        compiler_params=_params(("parallel", "arbitrary")),
        name="ffn_in",
    )(hp, hp, hs, w_in, w_in, cw, cb.reshape(1, dff))
    return hidden, tails[tiles_per_seq - 1::tiles_per_seq], gs, us


def _ffn_act_sample_kernel(g_ref, u_ref, cache_ref, cw_ref, cb_ref, h_ref, nb_ref, *, ts):
    w = FFN_CONV_WIDTH
    rows = [cache_ref[0, 0, r:r + 1, :] for r in range(w - 1)] + \
           [g_ref[0, t:t + 1, :] for t in range(ts)]
    for t in range(ts):
        gc = cb_ref[...]
        for j in range(w):
            gc = gc + cw_ref[j:j + 1, :] * rows[t + j]
        h_ref[0, t:t + 1, :] = gc * _sigmoid(gc) * u_ref[0, t:t + 1, :]
    for r in range(w - 1):
        nb_ref[0, r:r + 1, :] = rows[ts + r]


def _ffn_act_sample(gs3, us3, cache, layer, cw, cb):
    nb, ts, dff = gs3.shape
    w = FFN_CONV_WIDTH
    row = lambda: pl.BlockSpec((1, ts, dff), lambda b: (b, 0, 0))
    return pl.pallas_call(
        functools.partial(_ffn_act_sample_kernel, ts=ts),
        grid=(nb,),
        in_specs=[row(), row(),
                  pl.BlockSpec((1, 1, w - 1, dff), lambda b: (layer, b, 0, 0)),
                  pl.BlockSpec((w, dff), lambda b: (0, 0)),
                  pl.BlockSpec((1, dff), lambda b: (0, 0))],
        out_specs=[row(), pl.BlockSpec((1, w - 1, dff), lambda b: (b, 0, 0))],
        out_shape=[jax.ShapeDtypeStruct((nb, ts, dff), F32),
                   jax.ShapeDtypeStruct((nb, w - 1, dff), F32)],
        compiler_params=_params(("parallel",)),
        name="ffn_act_sample",
    )(gs3, us3, cache, cw, cb.reshape(1, dff))


def kernel(x_prompt, x_sample, mem_prompt, cache_moba_k, cache_moba_v, page_table, cache_convmod,
           state_hgrn, cache_mem_k, cache_mem_v, cache_ffn_conv, g_mix, g_mem_q, g_mem_kv, g_ffn,
           g_final, w_in_a, w_out_a, conv_w_b, conv_b_b, ln_g_b, ln_b_b, w_in_c, w_out_c,
           lower_bounds, gnorm_c, w_mem_q, w_mem_kv, w_mem_o, w_ffn_in, ffn_conv_w, ffn_conv_b,
           w_ffn_out):
    bp, tp, d = x_prompt.shape
    bs, ts, _ = x_sample.shape
    depth = g_mix.shape[0]
    n_mem = mem_prompt.shape[1]
    n_pages = page_table.shape[1]
    n_pool = cache_moba_k.shape[1]
    dff = w_ffn_out.shape[1]
    b_ch = conv_w_b.shape[2]
    c_heads = w_out_c.shape[1] // C_KEY_DIM
    mw = w_mem_q.shape[2]
    mp, ms = bp * tp, bs * ts

    xp = x_prompt.reshape(mp, d)
    xs = x_sample.reshape(ms, d)
    pt_flat = page_table.reshape(-1)
    mem2 = mem_prompt.reshape(bp * n_mem, d)
    ck = cache_moba_k.reshape(-1, PAGE_SIZE, A_WIDTH)
    cv = cache_moba_v.reshape(-1, PAGE_SIZE, A_WIDTH)
    cmk = cache_mem_k.reshape(depth, bs, n_mem, mw)
    cmv = cache_mem_v.reshape(depth, bs, n_mem, mw)
    w_ffn_out_bf = w_ffn_out.astype(BF16)

    outs = {k: [] for k in ("mk_p", "mv_p", "mk_s", "mv_s", "cv_p", "cv_s", "hg_p", "hg_s",
                            "memk", "memv", "ff_p", "ff_s")}
    for l in range(depth):
        j = l // 2
        hp = _rmsnorm(xp, g_mix[l], BF16)
        hs = _rmsnorm(xs, g_mix[l], F32)
        if l % 2 == 0:
            qp, qs = _proj(hp, hs, w_in_a, j, n_off=0, n=A_WIDTH, name="in_a_q")
            kp, ks = _proj(hp, hs, w_in_a, j, n_off=A_WIDTH, n=A_WIDTH, name="in_a_k")
            vp, vs = _proj(hp, hs, w_in_a, j, n_off=2 * A_WIDTH, n=A_WIDTH, name="in_a_v")
            glu_p, glu_s = _proj(hp, hs, w_in_a, j, n_off=3 * A_WIDTH, n=2 * b_ch, name="in_a_glu")
            qs3 = qs.reshape(bs, ts, A_WIDTH)
            cat_p = _moba_prompt(qp, kp, vp, bp, tp, A_WIDTH + b_ch)
            cat_p, buf_p = _convmod_prompt(glu_p, cat_p, bp, tp, conv_w_b[j], conv_b_b[j],
                                           ln_g_b[j], ln_b_b[j])
            kmean = _moba_kmean(ck, pt_flat, j * n_pool, bs, n_pages)
            sel = _moba_select(qs3, kmean.reshape(bs, -1, A_WIDTH))
            oa_s = _moba_sample(qs3, ks.reshape(bs, ts, A_WIDTH), vs.reshape(bs, ts, A_WIDTH),
                                ck, cv, sel.reshape(-1), pt_flat, j * n_pool, n_pages)
            c_s, buf_s = _convmod_sample(glu_s.reshape(bs, ts, 2 * b_ch), cache_convmod[j],
                                         conv_w_b[j], conv_b_b[j], ln_g_b[j], ln_b_b[j])
            cat_s = jnp.concatenate([oa_s.reshape(ms, A_WIDTH), c_s.reshape(ms, b_ch)], axis=1)
            xp, xs = _proj(cat_p, cat_s, w_out_a, j, res=(xp, xs), name="out_a")
            outs["mk_p"].append(kp.reshape(bp, tp, A_HEADS, A_HEAD_DIM))
            outs["mv_p"].append(vp.reshape(bp, tp, A_HEADS, A_HEAD_DIM))
            outs["mk_s"].append(ks.reshape(bs, ts, A_HEADS, A_HEAD_DIM))
            outs["mv_s"].append(vs.reshape(bs, ts, A_HEADS, A_HEAD_DIM))
            outs["cv_p"].append(buf_p)
            outs["cv_s"].append(buf_s)
        else:
            zp, zs = _proj(hp, hs, w_in_c, j, name="in_c")
            og_p, st_p = _hgrn_prompt(zp, bp, tp, c_heads, lower_bounds, gnorm_c[j], l)
            og_s, st_s = _hgrn_sample(zs.reshape(bs, ts, -1), c_heads, lower_bounds, gnorm_c[j],
                                      state_hgrn, j)
            xp, xs = _proj(og_p, og_s.reshape(ms, -1), w_out_c, j, res=(xp, xs), name="out_c")
            outs["hg_p"].append(st_p)
            outs["hg_s"].append(st_s)

        hm = _rmsnorm(mem2, g_mem_kv[l], BF16)
        mk = _mm(hm, w_mem_kv, l, tm=bp * n_mem, tn=PROJ_TN, n_off=0, n=mw, name="mem_k")
        mv = _mm(hm, w_mem_kv, l, tm=bp * n_mem, tn=PROJ_TN, n_off=mw, n=mw, name="mem_v")
        hp = _rmsnorm(xp, g_mem_q[l], BF16)
        hs = _rmsnorm(xs, g_mem_q[l], F32)
        qp, qs = _proj(hp, hs, w_mem_q, l, out_dtype=BF16, name="mem_q")
        ap = _mem_attn_prompt(qp, mk, mv, bp, tp, n_mem)
        a_s = _mem_attn_sample(qs.reshape(bs, ts, mw), cmk, cmv, l)
        xp, xs = _proj(ap, a_s.reshape(ms, mw), w_mem_o, l, res=(xp, xs), name="mem_o")
        outs["memk"].append(mk.reshape(bp, n_mem, MEM_HEADS, mw // MEM_HEADS))
        outs["memv"].append(mv.reshape(bp, n_mem, MEM_HEADS, mw // MEM_HEADS))

        hp = _rmsnorm(xp, g_ffn[l], BF16)
        hs = _rmsnorm(xs, g_ffn[l], F32)
        hid_p, fb_p, gs, us = _ffn_in(hp, hs, w_ffn_in, l, ffn_conv_w[l], ffn_conv_b[l], tp, dff)
        hid_s, fb_s = _ffn_act_sample(gs.reshape(bs, ts, dff), us.reshape(bs, ts, dff),
                                      cache_ffn_conv, l, ffn_conv_w[l], ffn_conv_b[l])
        xp = _mm(hid_p, w_ffn_out_bf, l, tm=PROMPT_TM, tn=PROJ_TN, tk=dff // 2, res=xp,
                 name="ffn_out_prompt")
        xs = _mm(hid_s.reshape(ms, dff), w_ffn_out_bf, l, tm=ms, tn=PROJ_TN, tk=dff // 2, res=xs,
                 name="ffn_out_sample")
        outs["ff_p"].append(fb_p)
        outs["ff_s"].append(fb_s)

    y_prompt = _rmsnorm(xp, g_final, F32).reshape(bp, tp, d)
    y_sample = _rmsnorm(xs, g_final, F32).reshape(bs, ts, d)
    st = jnp.stack
    return (y_prompt, y_sample, st(outs["mk_p"]), st(outs["mv_p"]), st(outs["mk_s"]), st(outs["mv_s"]),
            st(outs["cv_p"]), st(outs["cv_s"]), st(outs["hg_p"]), st(outs["hg_s"]),
            st(outs["memk"]), st(outs["memv"]), st(outs["ff_p"]), st(outs["ff_s"]))
```

```python
import functools
import math

import jax
import jax.numpy as jnp
from jax import lax
from jax.experimental import pallas as pl
from jax.experimental.pallas import tpu as pltpu

F32 = jnp.float32
BF16 = jnp.bfloat16

EPS = 1e-6
NEG = -1e30

PAGE_SIZE = 128
A_HEADS = 16
A_HEAD_DIM = 128
A_WIDTH = A_HEADS * A_HEAD_DIM
MOBA_BLOCK = 256
MOBA_TOPK = 3
B_CONV_WIDTH = 31
C_KEY_DIM = 128
MEM_HEADS = 4
FFN_CONV_WIDTH = 3

LANES = 128
F32_SUBLANES = 8
BF16_SUBLANES = 16
VMEM_LIMIT = 56 * 1024 * 1024

NT_DIMS = (((1,), (1,)), ((), ()))
TN_DIMS = (((0,), (0,)), ((), ()))


def _params(sem, vmem=VMEM_LIMIT):
    return pltpu.CompilerParams(dimension_semantics=sem, vmem_limit_bytes=vmem)


def _sigmoid(x):
    return 1.0 / (1.0 + jnp.exp(-x))


def _split3(a):
    a1 = a.astype(BF16)
    r1 = a - a1.astype(F32)
    a2 = r1.astype(BF16)
    a3 = (r1 - a2.astype(F32)).astype(BF16)
    return a1, a2, a3


def _dot_f32(a, b, dims):
    a1, a2, a3 = _split3(a)
    b1, b2, b3 = _split3(b)
    d = lambda x, y: lax.dot_general(x, y, dims, preferred_element_type=F32)
    low = d(a1, b3) + d(a3, b1) + d(a2, b2)
    mid = d(a1, b2) + d(a2, b1)
    return (low + mid) + d(a1, b1)


def _rmsnorm_kernel(x_ref, g_ref, o_ref):
    x = x_ref[...]
    ms = jnp.mean(x * x, axis=-1, keepdims=True)
    o_ref[...] = (x * lax.rsqrt(ms + EPS) * g_ref[...]).astype(o_ref.dtype)


def _rmsnorm(x, g, out_dtype):
    m, d = x.shape
    tm = min(m, 512)
    return pl.pallas_call(
        _rmsnorm_kernel,
        grid=(m // tm,),
        in_specs=[pl.BlockSpec((tm, d), lambda i: (i, 0)),
                  pl.BlockSpec((1, d), lambda i: (0, 0))],
        out_specs=pl.BlockSpec((tm, d), lambda i: (i, 0)),
        out_shape=jax.ShapeDtypeStruct((m, d), out_dtype),
        compiler_params=_params(("parallel",)),
        name="rmsnorm",
    )(x, g.reshape(1, d))


PROMPT_TM = 1024
PROJ_TN = 512


def _proj_kernel(*refs, k_parts, has_res):
    nparts = len(k_parts)
    xp_refs = refs[:nparts]
    if has_res:
        xs_ref, w_ref, rp_ref, rs_ref, yp_ref, ys_ref, wb = refs[nparts:]
    else:
        xs_ref, w_ref, yp_ref, ys_ref, wb = refs[nparts:]

    @pl.when(pl.program_id(1) == 0)
    def _():
        wb[...] = w_ref[0].astype(BF16)
        ys = jnp.dot(xs_ref[...].astype(BF16), wb[...], preferred_element_type=F32)
        if has_res:
            ys = ys + rs_ref[...]
        ys_ref[...] = ys.astype(ys_ref.dtype)

    acc = None
    k0 = 0
    for xp_ref, kw in zip(xp_refs, k_parts):
        part = jnp.dot(xp_ref[...], wb[k0:k0 + kw, :], preferred_element_type=F32)
        acc = part if acc is None else acc + part
        k0 += kw
    if has_res:
        acc = acc + rp_ref[...]
    yp_ref[...] = acc.astype(yp_ref.dtype)


def _proj(xp_parts, xs, w, layer, *, n_off=0, n=None, res=None, out_dtype=F32, name="proj"):
    mp = xp_parts[0].shape[0]
    k_parts = tuple(x.shape[1] for x in xp_parts)
    kdim = sum(k_parts)
    ms = xs.shape[0]
    n = w.shape[2] if n is None else n
    tm, tn = PROMPT_TM, PROJ_TN
    assert mp % tm == 0 and n % tn == 0 and n_off % tn == 0 and w.shape[1] == kdim
    assert xs.shape[1] == kdim
    joff = n_off // tn
    in_specs = [pl.BlockSpec((tm, kw), lambda j, i: (i, 0)) for kw in k_parts]
    in_specs += [pl.BlockSpec((ms, kdim), lambda j, i: (0, 0)),
                 pl.BlockSpec((1, kdim, tn), lambda j, i: (layer, 0, j + joff))]
    args = list(xp_parts) + [xs, w]
    if res is not None:
        in_specs += [pl.BlockSpec((tm, tn), lambda j, i: (i, j)),
                     pl.BlockSpec((ms, tn), lambda j, i: (0, j))]
        args += list(res)
    return pl.pallas_call(
        functools.partial(_proj_kernel, k_parts=k_parts, has_res=res is not None),
        grid=(n // tn, mp // tm),
        in_specs=in_specs,
        out_specs=[pl.BlockSpec((tm, tn), lambda j, i: (i, j)),
                   pl.BlockSpec((ms, tn), lambda j, i: (0, j))],
        out_shape=[jax.ShapeDtypeStruct((mp, n), out_dtype),
                   jax.ShapeDtypeStruct((ms, n), F32)],
        scratch_shapes=[pltpu.VMEM((kdim, tn), BF16)],
        compiler_params=_params(("parallel", "arbitrary")),
        name=name,
    )(*args)


def _mm_kernel(x_ref, w_ref, *rest, nk, has_res):
    if has_res:
        r_ref, o_ref = rest
    else:
        (o_ref,) = rest
    acc = jnp.dot(x_ref[...].astype(BF16), w_ref[0].astype(BF16), preferred_element_type=F32)
    if nk == 1:
        if has_res:
            acc = acc + r_ref[...]
        o_ref[...] = acc.astype(o_ref.dtype)
    else:
        k = pl.program_id(2)

        @pl.when(k == 0)
        def _():
            o_ref[...] = (acc + r_ref[...]) if has_res else acc

        @pl.when(k > 0)
        def _():
            o_ref[...] += acc


def _mm(x, w, layer, *, tm, tn, tk=None, n_off=0, n=None, res=None, out_dtype=F32, name="matmul"):
    m, kdim = x.shape
    n = w.shape[2] if n is None else n
    tk = kdim if tk is None else tk
    nk = kdim // tk
    assert m % tm == 0 and n % tn == 0 and kdim % tk == 0 and n_off % tn == 0
    assert nk == 1 or out_dtype == F32
    joff = n_off // tn
    in_specs = [pl.BlockSpec((tm, tk), lambda i, j, k: (i, k)),
                pl.BlockSpec((1, tk, tn), lambda i, j, k: (layer, k, j + joff))]
    args = [x, w]
    if res is not None:
        in_specs.append(pl.BlockSpec((tm, tn), lambda i, j, k: (i, j)))
        args.append(res)
    return pl.pallas_call(
        functools.partial(_mm_kernel, nk=nk, has_res=res is not None),
        grid=(m // tm, n // tn, nk),
        in_specs=in_specs,
        out_specs=pl.BlockSpec((tm, tn), lambda i, j, k: (i, j)),
        out_shape=jax.ShapeDtypeStruct((m, n), out_dtype),
        compiler_params=_params(("parallel", "parallel", "arbitrary")),
        name=name,
    )(*args)


def _moba_prompt_kernel(q_ref, k_ref, v_ref, o_ref, *, nblk):
    blk = MOBA_BLOCK
    scale = 1.0 / math.sqrt(A_HEAD_DIM)
    k = k_ref[...]
    kb = k.astype(BF16)
    vb = v_ref[...].astype(BF16)
    kmean = jnp.mean(k.reshape(nblk, blk, A_HEAD_DIM), axis=1)
    krow = lax.broadcasted_iota(jnp.int32, (blk, blk), 0)
    qcol = lax.broadcasted_iota(jnp.int32, (blk, blk), 1)
    causal = krow <= qcol
    for qb in range(nblk):
        qf = q_ref[qb * blk:(qb + 1) * blk, :]
        qbf = (qf * scale).astype(BF16)
        sel = None
        if qb > MOBA_TOPK:
            gate = _dot_f32(kmean, qf, NT_DIMS)
            rows = [gate[n:n + 1, :] for n in range(qb)]
            sel = []
            for n in range(qb):
                rank = jnp.zeros((1, blk), jnp.int32)
                for n2 in range(qb):
                    if n2 == n:
                        continue
                    beats = (rows[n2] >= rows[n]) if n2 < n else (rows[n2] > rows[n])
                    rank = rank + beats.astype(jnp.int32)
                sel.append(rank < MOBA_TOPK)
        s_list = []
        for n in range(qb + 1):
            s = lax.dot_general(kb[n * blk:(n + 1) * blk], qbf, NT_DIMS,
                                preferred_element_type=F32)
            if n == qb:
                s = jnp.where(causal, s, NEG)
            elif sel is not None:
                s = jnp.where(sel[n], s, NEG)
            s_list.append(s)
        m = s_list[0].max(axis=0, keepdims=True)
        for s in s_list[1:]:
            m = jnp.maximum(m, s.max(axis=0, keepdims=True))
        l = jnp.zeros((1, blk), F32)
        ot = jnp.zeros((A_HEAD_DIM, blk), F32)
        for n, s in enumerate(s_list):
            p = jnp.exp(s - m)
            l = l + p.sum(axis=0, keepdims=True)
            ot = ot + lax.dot_general(vb[n * blk:(n + 1) * blk], p.astype(BF16), TN_DIMS,
                                      preferred_element_type=F32)
        o_ref[qb * blk:(qb + 1) * blk, :] = (ot / l).T.astype(o_ref.dtype)


def _moba_prompt(q, k, v, nb, t):
    hd = A_HEAD_DIM
    spec = lambda: pl.BlockSpec((t, hd), lambda b, h: (b, h))
    return pl.pallas_call(
        functools.partial(_moba_prompt_kernel, nblk=t // MOBA_BLOCK),
        grid=(nb, A_HEADS),
        in_specs=[spec(), spec(), spec()],
        out_specs=spec(),
        out_shape=jax.ShapeDtypeStruct((nb * t, A_WIDTH), BF16),
        compiler_params=_params(("parallel", "parallel")),
        name="moba_prompt",
    )(q, k, v)


PAGES_PER_BLOCK = MOBA_BLOCK // PAGE_SIZE
KMEAN_BLOCKS = 2


def _kmean_kernel(pt_ref, *refs):
    o_ref = refs[-1]
    for i in range(KMEAN_BLOCKS):
        s = jnp.zeros((A_HEADS, A_HEAD_DIM), F32)
        for pg in range(PAGES_PER_BLOCK):
            s = s + jnp.sum(refs[i * PAGES_PER_BLOCK + pg][0, 0], axis=0)
        o_ref[0, i] = s * (1.0 / MOBA_BLOCK)


def _moba_kmean(cache_k, pt_flat, j, nb, n_pages):
    nblk = n_pages // PAGES_PER_BLOCK
    per_step = KMEAN_BLOCKS * PAGES_PER_BLOCK
    page = lambda pg: pl.BlockSpec(
        (1, 1, PAGE_SIZE, A_HEADS, A_HEAD_DIM),
        lambda b, n, pt: (j, pt[b * n_pages + per_step * n + pg], 0, 0, 0))
    return pl.pallas_call(
        _kmean_kernel,
        grid_spec=pltpu.PrefetchScalarGridSpec(
            num_scalar_prefetch=1,
            grid=(nb, nblk // KMEAN_BLOCKS),
            in_specs=[page(pg) for pg in range(per_step)],
            out_specs=pl.BlockSpec((1, KMEAN_BLOCKS, A_HEADS, A_HEAD_DIM),
                                   lambda b, n, pt: (b, n, 0, 0)),
        ),
        out_shape=jax.ShapeDtypeStruct((nb, nblk, A_HEADS, A_HEAD_DIM), F32),
        compiler_params=_params(("parallel", "parallel")),
        name="moba_kmean",
    )(pt_flat, *([cache_k] * per_step))


def _moba_select_kernel(q_ref, km_ref, o_ref, *, ts, nblk):
    q = q_ref[0]
    km = km_ref[0]
    nq = ts * A_HEADS
    qrep = jnp.concatenate(
        [jnp.broadcast_to(q[t:t + 1], (A_HEADS, A_WIDTH)) for t in range(ts)], axis=0)
    rowh = lax.broadcasted_iota(jnp.int32, (nq, A_WIDTH), 0) % A_HEADS
    colh = lax.broadcasted_iota(jnp.int32, (nq, A_WIDTH), 1) // A_HEAD_DIM
    qbig = jnp.where(rowh == colh, qrep, 0.0)
    gate = _dot_f32(km, qbig, NT_DIMS)
    nidx = lax.broadcasted_iota(jnp.int32, (nblk, nq), 0)
    for r in range(MOBA_TOPK):
        m = gate.max(axis=0, keepdims=True)
        idx = jnp.min(jnp.where(gate == m, nidx, nblk), axis=0, keepdims=True)
        o_ref[0, r:r + 1, :] = idx
        gate = jnp.where(nidx == idx, -jnp.inf, gate)


def _moba_select(qs3, kmean):
    nb, ts, _ = qs3.shape
    nblk = kmean.shape[1]
    nq = ts * A_HEADS
    return pl.pallas_call(
        functools.partial(_moba_select_kernel, ts=ts, nblk=nblk),
        grid=(nb,),
        in_specs=[pl.BlockSpec((1, ts, A_WIDTH), lambda b: (b, 0, 0)),
                  pl.BlockSpec((1, nblk, A_WIDTH), lambda b: (b, 0, 0))],
        out_specs=pl.BlockSpec((1, MOBA_TOPK, nq), lambda b: (b, 0, 0)),
        out_shape=jax.ShapeDtypeStruct((nb, MOBA_TOPK, nq), jnp.int32),
        compiler_params=_params(("parallel",)),
        name="moba_select",
    )(qs3, kmean)


def _moba_sample_kernel(sel_ref, pt_ref, q_ref, kn_ref, vn_ref, ck_hbm, cv_hbm, o_ref,
                        kbuf, vbuf, sems, *, j, ts, n_pages):
    b = pl.program_id(0)
    h = pl.program_id(1)
    nblk = n_pages // PAGES_PER_BLOCK
    nq = ts * A_HEADS
    step = b * A_HEADS + h
    nsteps = pl.num_programs(0) * A_HEADS
    half = step % 2

    def page_copies(bi, hi, hf):
        copies = []
        for t in range(ts):
            for r in range(MOBA_TOPK):
                blkid = jnp.minimum(sel_ref[bi * (MOBA_TOPK * nq) + r * nq + t * A_HEADS + hi], nblk - 1)
                for pg in range(PAGES_PER_BLOCK):
                    page = pt_ref[bi * n_pages + blkid * PAGES_PER_BLOCK + pg]
                    slot = (t * MOBA_TOPK + r) * PAGES_PER_BLOCK + pg
                    copies.append(pltpu.make_async_copy(
                        ck_hbm.at[j, page, :, hi, :], kbuf.at[hf, slot], sems.at[hf]))
                    copies.append(pltpu.make_async_copy(
                        cv_hbm.at[j, page, :, hi, :], vbuf.at[hf, slot], sems.at[hf]))
        return copies

    @pl.when(step == 0)
    def _():
        for c in page_copies(b, h, half):
            c.start()

    @pl.when(step + 1 < nsteps)
    def _():
        nxt = step + 1
        for c in page_copies(nxt // A_HEADS, nxt % A_HEADS, 1 - half):
            c.start()

    for c in page_copies(b, h, half):
        c.wait()

    scale = 1.0 / math.sqrt(A_HEAD_DIM)
    npg = MOBA_TOPK * PAGES_PER_BLOCK
    trow = lax.broadcasted_iota(jnp.int32, (ts, 1), 0)
    for t in range(ts):
        q = q_ref[0, t:t + 1, :] * scale
        slots = range(t * npg, (t + 1) * npg)
        s_cols = [jnp.sum(kbuf[half, sl] * q, axis=1, keepdims=True) for sl in slots]
        s_own = jnp.sum(kn_ref[0] * q, axis=1, keepdims=True)
        s_own = jnp.where(trow <= t, s_own, NEG)
        m = s_own.max(axis=0, keepdims=True)
        for s in s_cols:
            m = jnp.maximum(m, s.max(axis=0, keepdims=True))
        p_own = jnp.exp(s_own - m)
        l = p_own.sum(axis=0, keepdims=True)
        o = jnp.sum(p_own * vn_ref[0], axis=0, keepdims=True)
        for s, sl in zip(s_cols, slots):
            p = jnp.exp(s - m)
            l = l + p.sum(axis=0, keepdims=True)
            o = o + jnp.sum(p * vbuf[half, sl], axis=0, keepdims=True)
        o_ref[0, t:t + 1, :] = o / l


def _moba_sample(qs3, ks3, vs3, cache_k, cache_v, sel_flat, pt_flat, j, n_pages):
    nb, ts, _ = qs3.shape
    hd = A_HEAD_DIM
    nslot = ts * MOBA_TOPK * PAGES_PER_BLOCK
    own = lambda: pl.BlockSpec((1, ts, hd), lambda b, h, sel, pt: (b, 0, h))
    hbm = lambda: pl.BlockSpec(memory_space=pl.ANY)
    return pl.pallas_call(
        functools.partial(_moba_sample_kernel, j=j, ts=ts, n_pages=n_pages),
        grid_spec=pltpu.PrefetchScalarGridSpec(
            num_scalar_prefetch=2,
            grid=(nb, A_HEADS),
            in_specs=[own(), own(), own(), hbm(), hbm()],
            out_specs=own(),
            scratch_shapes=[pltpu.VMEM((2, nslot, PAGE_SIZE, hd), F32),
                            pltpu.VMEM((2, nslot, PAGE_SIZE, hd), F32),
                            pltpu.SemaphoreType.DMA((2,))],
        ),
        out_shape=jax.ShapeDtypeStruct((nb, ts, A_WIDTH), F32),
        compiler_params=_params(("arbitrary", "arbitrary")),
        name="moba_sample",
    )(sel_flat, pt_flat, qs3, ks3, vs3, cache_k, cache_v)


def _layernorm_swish(y, g, b):
    mu = jnp.mean(y, axis=-1, keepdims=True)
    yc = y - mu
    var = jnp.mean(yc * yc, axis=-1, keepdims=True)
    yn = yc * lax.rsqrt(var + EPS) * g + b
    return yn * _sigmoid(yn)


CONV_HALO = 32
CONV_ROWS = 128
CONV_COLS = 128


def _convmod_prompt_kernel(ga_ref, gb_ref, cw_ref, cb_ref, lg_ref, lb_ref, c_ref, nb_ref,
                           ubuf, ybuf, wbuf, *, tt, nt):
    w = B_CONV_WIDTH
    ch = ubuf.shape[1]
    t = pl.program_id(1)

    @pl.when(t == 0)
    def _():
        ubuf[0:CONV_HALO, :] = jnp.zeros((CONV_HALO, ch), F32)

    ubuf[CONV_HALO:CONV_HALO + tt, :] = ga_ref[...] * _sigmoid(gb_ref[...])
    base = CONV_HALO - (w - 1)
    for c0 in range(0, ch, CONV_COLS):
        cols = slice(c0, c0 + CONV_COLS)
        for r0 in range(0, tt, CONV_ROWS):
            acc = jnp.broadcast_to(cb_ref[:, cols], (CONV_ROWS, CONV_COLS))
            for s in range(F32_SUBLANES):
                taps = range(s, w, F32_SUBLANES)
                start = r0 + base + s
                rows = CONV_ROWS + F32_SUBLANES * (len(taps) - 1)
                wbuf[s, 0:rows, :] = ubuf[start:start + rows, cols]
                for a, j in enumerate(taps):
                    acc = acc + cw_ref[j:j + 1, cols] * \
                        wbuf[s, a * F32_SUBLANES:a * F32_SUBLANES + CONV_ROWS, :]
            ybuf[r0:r0 + CONV_ROWS, cols] = acc
    c_ref[...] = _layernorm_swish(ybuf[...], lg_ref[...], lb_ref[...]).astype(c_ref.dtype)

    @pl.when(t == nt - 1)
    def _():
        nb_ref[0] = ubuf[CONV_HALO + tt - (w - 1):CONV_HALO + tt, :]

    ubuf[0:CONV_HALO, :] = ubuf[tt:tt + CONV_HALO, :]


def _convmod_prompt(glu, nb, t, cw, cb, lg, lb):
    ch = glu.shape[1] // 2
    tt = 256
    nt = t // tt
    vec = lambda: pl.BlockSpec((1, ch), lambda b, i: (0, 0))
    return pl.pallas_call(
        functools.partial(_convmod_prompt_kernel, tt=tt, nt=nt),
        grid=(nb, nt),
        in_specs=[pl.BlockSpec((tt, ch), lambda b, i: (b * nt + i, 0)),
                  pl.BlockSpec((tt, ch), lambda b, i: (b * nt + i, 1)),
                  pl.BlockSpec((B_CONV_WIDTH, ch), lambda b, i: (0, 0)),
                  vec(), vec(), vec()],
        out_specs=[pl.BlockSpec((tt, ch), lambda b, i: (b * nt + i, 0)),
                   pl.BlockSpec((1, B_CONV_WIDTH - 1, ch), lambda b, i: (b, 0, 0))],
        out_shape=[jax.ShapeDtypeStruct((nb * t, ch), BF16),
                   jax.ShapeDtypeStruct((nb, B_CONV_WIDTH - 1, ch), F32)],
        scratch_shapes=[pltpu.VMEM((CONV_HALO + tt, ch), F32), pltpu.VMEM((tt, ch), F32),
                        pltpu.VMEM((F32_SUBLANES, CONV_ROWS + CONV_HALO, CONV_COLS), F32)],
        compiler_params=_params(("parallel", "arbitrary")),
        name="convmod_prompt",
    )(glu, glu, cw, cb.reshape(1, ch), lg.reshape(1, ch), lb.reshape(1, ch))


def _convmod_sample_kernel(ga_ref, gb_ref, cache_ref, cw_ref, cb_ref, lg_ref, lb_ref,
                           c_ref, nb_ref, ubuf, *, ts):
    w = B_CONV_WIDTH
    ch = ubuf.shape[1]
    ubuf[0:w - 1, :] = cache_ref[0]
    ubuf[w - 1:w - 1 + ts, :] = ga_ref[0] * _sigmoid(gb_ref[0])
    acc = jnp.broadcast_to(cb_ref[...], (ts, ch))
    for j in range(w):
        acc = acc + cw_ref[j:j + 1, :] * ubuf[j:j + ts, :]
    c_ref[0] = _layernorm_swish(acc, lg_ref[...], lb_ref[...])
    nb_ref[0] = ubuf[ts:ts + w - 1, :]


def _convmod_sample(glu3, cache, cw, cb, lg, lb):
    nb, ts, ch2 = glu3.shape
    ch = ch2 // 2
    w = B_CONV_WIDTH
    vec = lambda: pl.BlockSpec((1, ch), lambda b: (0, 0))
    return pl.pallas_call(
        functools.partial(_convmod_sample_kernel, ts=ts),
        grid=(nb,),
        in_specs=[pl.BlockSpec((1, ts, ch), lambda b: (b, 0, 0)),
                  pl.BlockSpec((1, ts, ch), lambda b: (b, 0, 1)),
                  pl.BlockSpec((1, w - 1, ch), lambda b: (b, 0, 0)),
                  pl.BlockSpec((w, ch), lambda b: (0, 0)),
                  vec(), vec(), vec()],
        out_specs=[pl.BlockSpec((1, ts, ch), lambda b: (b, 0, 0)),
                   pl.BlockSpec((1, w - 1, ch), lambda b: (b, 0, 0))],
        out_shape=[jax.ShapeDtypeStruct((nb, ts, ch), F32),
                   jax.ShapeDtypeStruct((nb, w - 1, ch), F32)],
        scratch_shapes=[pltpu.VMEM((w - 1 + ts + 5, ch), F32)],
        compiler_params=_params(("parallel",)),
        name="convmod_sample",
    )(glu3, glu3, cache, cw, cb.reshape(1, ch), lg.reshape(1, ch), lb.reshape(1, ch))


HG_CHUNK = 128
HG_SUB = 16
HG_HEADS = 4


def _hgrn_lower_bound(lbp, layer):
    e = jnp.exp(lbp - lbp.max(axis=0, keepdims=True))
    sm = e / e.sum(axis=0, keepdims=True)
    lb = jnp.zeros_like(sm[0:1])
    for i in range(1, layer + 1):
        lb = lb + sm[i:i + 1]
    return lb


def _hgrn_out(o, g, gn):
    ms = jnp.mean(o * o, axis=-1, keepdims=True)
    return (o * lax.rsqrt(ms + EPS) * gn) * (g * _sigmoid(g))


def _hgrn_chunk(q, f, v, g, lb, gn, s_prev, kpad, bpad, consts):
    cs, sub, kd = HG_CHUNK, HG_SUB, C_KEY_DIM
    tril, rowk, eye, lane, blockdiag = consts
    fg = lb + (1.0 - lb) * _sigmoid(f)
    kk = 1.0 - fg
    vb = v.astype(BF16)
    l1, l2, l3 = _split3(jnp.log(fg))
    cum = lambda x: jnp.dot(tril, x, preferred_element_type=F32)
    b = (cum(l3) + cum(l2)) + cum(l1)

    o = jnp.dot((q * jnp.exp(b)).astype(BF16), s_prev.astype(BF16), preferred_element_type=F32)

    bref_rows = jnp.concatenate(
        [jnp.zeros((sub, kd), F32)]
        + [jnp.broadcast_to(b[sub * i - 1:sub * i], (sub, kd)) for i in range(1, cs // sub)], axis=0)
    qe = (q * jnp.exp(b - bref_rows)).astype(BF16)
    parts = [jnp.zeros((sub, cs), F32)]
    for i in range(1, cs // sub):
        bref = b[sub * i - 1:sub * i]
        ke = jnp.where(rowk < sub * i, kk * jnp.exp(jnp.minimum(bref - b, 0.0)), 0.0).astype(BF16)
        parts.append(lax.dot_general(qe[sub * i:sub * (i + 1)], ke, NT_DIMS,
                                     preferred_element_type=F32))
    a = jnp.concatenate(parts, axis=0)

    kpad[sub:, :] = kk
    bpad[sub:, :] = b
    band = jnp.zeros((cs, cs), F32)
    for d in range(sub):
        if d == 0:
            p = q * kk
        else:
            p = q * kpad[sub - d:sub - d + cs, :] * jnp.exp(b - bpad[sub - d:sub - d + cs, :])
        band = jnp.where(lane == (cs - d) % cs, p.sum(axis=1, keepdims=True), band)
    band = pltpu.roll(band, 0, 1, stride=1, stride_axis=0)
    a = a + jnp.where(blockdiag, band, 0.0)
    o = o + jnp.dot(a.astype(BF16), vb, preferred_element_type=F32)

    bl = b[cs - 1:cs]
    kd_mat = (kk * jnp.exp(bl - b)).astype(BF16)
    decay_col = jnp.sum(jnp.where(eye, jnp.broadcast_to(jnp.exp(bl), (kd, kd)), 0.0),
                        axis=1, keepdims=True)
    s_new = decay_col * s_prev + lax.dot_general(kd_mat, vb, TN_DIMS, preferred_element_type=F32)
    return _hgrn_out(o, g, gn), s_new


def _hgrn_prompt_kernel(q_ref, f_ref, i_ref, g_ref, lbp_ref, gn_ref, o_ref, s_ref,
                        kpad, bpad, *, layer):
    cs, sub, kd = HG_CHUNK, HG_SUB, C_KEY_DIM

    @pl.when(pl.program_id(2) == 0)
    def _():
        s_ref[...] = jnp.zeros_like(s_ref)

    rr = lax.broadcasted_iota(jnp.int32, (cs, cs), 0)
    cc = lax.broadcasted_iota(jnp.int32, (cs, cs), 1)
    consts = (jnp.where(cc <= rr, 1.0, 0.0).astype(BF16),
              lax.broadcasted_iota(jnp.int32, (cs, kd), 0),
              rr == cc,
              cc,
              (rr // sub == cc // sub) & (cc <= rr))
    kpad[:, 0:sub, :] = jnp.zeros((HG_HEADS, sub, kd), F32)
    bpad[:, 0:sub, :] = jnp.zeros((HG_HEADS, sub, kd), F32)
    lbs = _hgrn_lower_bound(lbp_ref[...], layer)
    for h in range(HG_HEADS):
        cols = slice(h * kd, (h + 1) * kd)
        og, s_new = _hgrn_chunk(q_ref[:, cols], f_ref[:, cols], i_ref[:, cols], g_ref[:, cols],
                                lbs[:, cols], gn_ref[...], s_ref[0, h],
                                kpad.at[h], bpad.at[h], consts)
        s_ref[0, h] = s_new
        o_ref[:, cols] = og.astype(o_ref.dtype)


def _hgrn_prompt(z, nb, t, heads, lower_bounds, gnorm, layer):
    kd = C_KEY_DIM
    assert HG_CHUNK == kd and heads % HG_HEADS == 0
    nchunk = t // HG_CHUNK
    depth = lower_bounds.shape[0]
    hw = HG_HEADS * kd
    ng = heads // HG_HEADS
    col = lambda part: pl.BlockSpec((HG_CHUNK, hw), lambda b, h, c: (b * nchunk + c, part * ng + h))
    return pl.pallas_call(
        functools.partial(_hgrn_prompt_kernel, layer=layer),
        grid=(nb, ng, nchunk),
        in_specs=[col(0), col(1), col(2), col(3),
                  pl.BlockSpec((depth, hw), lambda b, h, c: (0, h)),
                  pl.BlockSpec((1, kd), lambda b, h, c: (0, 0))],
        out_specs=[pl.BlockSpec((HG_CHUNK, hw), lambda b, h, c: (b * nchunk + c, h)),
                   pl.BlockSpec((1, HG_HEADS, kd, kd), lambda b, h, c: (b, h, 0, 0))],
        out_shape=[jax.ShapeDtypeStruct((nb * t, heads * kd), BF16),
                   jax.ShapeDtypeStruct((nb, heads, kd, kd), F32)],
        scratch_shapes=[pltpu.VMEM((HG_HEADS, HG_SUB + HG_CHUNK, kd), F32),
                        pltpu.VMEM((HG_HEADS, HG_SUB + HG_CHUNK, kd), F32)],
        compiler_params=_params(("parallel", "parallel", "arbitrary")),
        name="hgrn_prompt",
    )(z, z, z, z, lower_bounds, gnorm.reshape(1, kd))


def _hgrn_sample_kernel(q_ref, f_ref, i_ref, g_ref, lbp_ref, gn_ref, s0_ref, o_ref, s_ref,
                        *, layer, ts):
    kd = C_KEY_DIM
    lb = _hgrn_lower_bound(lbp_ref[...], layer)
    fg = lb + (1.0 - lb) * _sigmoid(f_ref[0])
    kk = 1.0 - fg
    q = q_ref[0]
    v = i_ref[0]
    g = g_ref[0]
    eye = lax.broadcasted_iota(jnp.int32, (kd, kd), 0) == lax.broadcasted_iota(jnp.int32, (kd, kd), 1)
    col = lambda x: jnp.sum(jnp.where(eye, jnp.broadcast_to(x, (kd, kd)), 0.0), axis=1, keepdims=True)
    s = s0_ref[0, 0]
    for t in range(ts):
        s = col(fg[t:t + 1]) * s + col(kk[t:t + 1]) * v[t:t + 1]
        o = jnp.sum(col(q[t:t + 1]) * s, axis=0, keepdims=True)
        o_ref[0, t:t + 1, :] = _hgrn_out(o, g[t:t + 1], gn_ref[...])
    s_ref[0, 0] = s


def _hgrn_sample(zs3, heads, lower_bounds, gnorm, state, j):
    nb, ts, _ = zs3.shape
    kd = C_KEY_DIM
    depth = lower_bounds.shape[0]
    st = state.reshape(-1, heads, kd, kd)
    col = lambda part: pl.BlockSpec((1, ts, kd), lambda b, h: (b, 0, part * heads + h))
    return pl.pallas_call(
        functools.partial(_hgrn_sample_kernel, layer=2 * j + 1, ts=ts),
        grid=(nb, heads),
        in_specs=[col(0), col(1), col(2), col(3),
                  pl.BlockSpec((depth, kd), lambda b, h: (0, h)),
                  pl.BlockSpec((1, kd), lambda b, h: (0, 0)),
                  pl.BlockSpec((1, 1, kd, kd), lambda b, h: (j * nb + b, h, 0, 0))],
        out_specs=[pl.BlockSpec((1, ts, kd), lambda b, h: (b, 0, h)),
                   pl.BlockSpec((1, 1, kd, kd), lambda b, h: (b, h, 0, 0))],
        out_shape=[jax.ShapeDtypeStruct((nb, ts, heads * kd), F32),
                   jax.ShapeDtypeStruct((nb, heads, kd, kd), F32)],
        compiler_params=_params(("parallel", "parallel")),
        name="hgrn_sample",
    )(zs3, zs3, zs3, zs3, lower_bounds, gnorm.reshape(1, kd), st)


def _softmax_attend(q, k, v):
    scale = 1.0 / math.sqrt(q.shape[-1])
    s = lax.dot_general(q.astype(BF16), k.astype(BF16), NT_DIMS, preferred_element_type=F32) * scale
    m = s.max(axis=1, keepdims=True)
    p = jnp.exp(s - m)
    l = p.sum(axis=1, keepdims=True)
    return jnp.dot(p.astype(BF16), v.astype(BF16), preferred_element_type=F32) / l


def _mem_attn_prompt_kernel(q_ref, k_ref, v_ref, o_ref):
    o_ref[...] = _softmax_attend(q_ref[...], k_ref[...], v_ref[...]).astype(o_ref.dtype)


def _mem_attn_prompt(q, mk, mv, nb, t, n_mem):
    w = q.shape[1]
    hd = w // MEM_HEADS
    tq = 512
    nt = t // tq
    return pl.pallas_call(
        _mem_attn_prompt_kernel,
        grid=(nb, MEM_HEADS, nt),
        in_specs=[pl.BlockSpec((tq, hd), lambda b, h, i: (b * nt + i, h)),
                  pl.BlockSpec((n_mem, hd), lambda b, h, i: (b, h)),
                  pl.BlockSpec((n_mem, hd), lambda b, h, i: (b, h))],
        out_specs=pl.BlockSpec((tq, hd), lambda b, h, i: (b * nt + i, h)),
        out_shape=jax.ShapeDtypeStruct((nb * t, w), BF16),
        compiler_params=_params(("parallel", "parallel", "parallel")),
        name="mem_attn_prompt",
    )(q, mk, mv)


def _mem_attn_sample_kernel(q_ref, k_ref, v_ref, o_ref):
    o_ref[0] = _softmax_attend(q_ref[0], k_ref[0, 0], v_ref[0, 0])


def _mem_attn_sample(q3, cache_k, cache_v, layer):
    nb, ts, w = q3.shape
    n_mem = cache_k.shape[2]
    hd = w // MEM_HEADS
    kv = lambda: pl.BlockSpec((1, 1, n_mem, hd), lambda b, h: (layer, b, 0, h))
    return pl.pallas_call(
        _mem_attn_sample_kernel,
        grid=(nb, MEM_HEADS),
        in_specs=[pl.BlockSpec((1, ts, hd), lambda b, h: (b, 0, h)), kv(), kv()],
        out_specs=pl.BlockSpec((1, ts, hd), lambda b, h: (b, 0, h)),
        out_shape=jax.ShapeDtypeStruct((nb, ts, w), F32),
        compiler_params=_params(("parallel", "parallel")),
        name="mem_attn_sample",
    )(q3, cache_k, cache_v)


FFN_TN = 256


def _ffn_in_kernel(xp_ref, xprev_ref, xs_ref, wg_ref, wu_ref, cw_ref, cb_ref,
                   h_ref, tail_ref, gs_ref, us_ref, wgb, wub, gbuf, *, tm, tiles_per_seq):
    i = pl.program_id(1)
    halo = BF16_SUBLANES

    @pl.when(i == 0)
    def _():
        wgb[...] = wg_ref[0].astype(BF16)
        wub[...] = wu_ref[0].astype(BF16)
        xs = xs_ref[...].astype(BF16)
        gs_ref[...] = jnp.dot(xs, wgb[...], preferred_element_type=F32)
        us_ref[...] = jnp.dot(xs, wub[...], preferred_element_type=F32)

    x = xp_ref[...]
    gate = jnp.dot(x, wgb[...], preferred_element_type=F32)
    up = jnp.dot(x, wub[...], preferred_element_type=F32)
    gprev = jnp.dot(xprev_ref[...], wgb[...], preferred_element_type=F32)
    gprev = jnp.where(i % tiles_per_seq == 0, 0.0, gprev)
    gbuf[0:halo, :] = gprev
    gbuf[halo:halo + tm, :] = gate
    cw = cw_ref[...]
    gc = (cw[0:1] * gbuf[halo - 2:halo - 2 + tm, :] + cw[1:2] * gbuf[halo - 1:halo - 1 + tm, :]
          + cw[2:3] * gate + cb_ref[...])
    h_ref[...] = (gc * _sigmoid(gc) * up).astype(h_ref.dtype)
    tail_ref[0] = gate[tm - (FFN_CONV_WIDTH - 1):tm, :]


def _ffn_in(hp, hs, w_in, layer, cw, cb, t, dff):
    m, d = hp.shape
    ms = hs.shape[0]
    tm, tn = PROMPT_TM, FFN_TN
    halo = BF16_SUBLANES
    tiles_per_seq = t // tm
    nj = dff // tn
    hidden, tails, gs, us = pl.pallas_call(
        functools.partial(_ffn_in_kernel, tm=tm, tiles_per_seq=tiles_per_seq),
        grid=(nj, m // tm),
        in_specs=[pl.BlockSpec((tm, d), lambda j, i: (i, 0)),
                  pl.BlockSpec((halo, d), lambda j, i: (jnp.maximum(i * (tm // halo) - 1, 0), 0)),
                  pl.BlockSpec((ms, d), lambda j, i: (0, 0)),
                  pl.BlockSpec((1, d, tn), lambda j, i: (layer, 0, j)),
                  pl.BlockSpec((1, d, tn), lambda j, i: (layer, 0, nj + j)),
                  pl.BlockSpec((FFN_CONV_WIDTH, tn), lambda j, i: (0, j)),
                  pl.BlockSpec((1, tn), lambda j, i: (0, j))],
        out_specs=[pl.BlockSpec((tm, tn), lambda j, i: (i, j)),
                   pl.BlockSpec((1, FFN_CONV_WIDTH - 1, tn), lambda j, i: (i, 0, j)),
                   pl.BlockSpec((ms, tn), lambda j, i: (0, j)),
                   pl.BlockSpec((ms, tn), lambda j, i: (0, j))],
        out_shape=[jax.ShapeDtypeStruct((m, dff), BF16),
                   jax.ShapeDtypeStruct((m // tm, FFN_CONV_WIDTH - 1, dff), F32),
                   jax.ShapeDtypeStruct((ms, dff), F32),
                   jax.ShapeDtypeStruct((ms, dff), F32)],
        scratch_shapes=[pltpu.VMEM((d, tn), BF16), pltpu.VMEM((d, tn), BF16),
                        pltpu.VMEM((halo + tm, tn), F32)],
        compiler_params=_params(("parallel", "arbitrary")),
        name="ffn_in",
    )(hp, hp, hs, w_in, w_in, cw, cb.reshape(1, dff))
    return hidden, tails[tiles_per_seq - 1::tiles_per_seq], gs, us


def _ffn_act_sample_kernel(g_ref, u_ref, cache_ref, cw_ref, cb_ref, h_ref, nb_ref, *, ts):
    w = FFN_CONV_WIDTH
    rows = [cache_ref[0, 0, r:r + 1, :] for r in range(w - 1)] + \
           [g_ref[0, t:t + 1, :] for t in range(ts)]
    for t in range(ts):
        gc = cb_ref[...]
        for j in range(w):
            gc = gc + cw_ref[j:j + 1, :] * rows[t + j]
        h_ref[0, t:t + 1, :] = gc * _sigmoid(gc) * u_ref[0, t:t + 1, :]
    for r in range(w - 1):
        nb_ref[0, r:r + 1, :] = rows[ts + r]


def _ffn_act_sample(gs3, us3, cache, layer, cw, cb):
    nb, ts, dff = gs3.shape
    w = FFN_CONV_WIDTH
    row = lambda: pl.BlockSpec((1, ts, dff), lambda b: (b, 0, 0))
    return pl.pallas_call(
        functools.partial(_ffn_act_sample_kernel, ts=ts),
        grid=(nb,),
        in_specs=[row(), row(),
                  pl.BlockSpec((1, 1, w - 1, dff), lambda b: (layer, b, 0, 0)),
                  pl.BlockSpec((w, dff), lambda b: (0, 0)),
                  pl.BlockSpec((1, dff), lambda b: (0, 0))],
        out_specs=[row(), pl.BlockSpec((1, w - 1, dff), lambda b: (b, 0, 0))],
        out_shape=[jax.ShapeDtypeStruct((nb, ts, dff), F32),
                   jax.ShapeDtypeStruct((nb, w - 1, dff), F32)],
        compiler_params=_params(("parallel",)),
        name="ffn_act_sample",
    )(gs3, us3, cache, cw, cb.reshape(1, dff))


def kernel(x_prompt, x_sample, mem_prompt, cache_moba_k, cache_moba_v, page_table, cache_convmod,
           state_hgrn, cache_mem_k, cache_mem_v, cache_ffn_conv, g_mix, g_mem_q, g_mem_kv, g_ffn,
           g_final, w_in_a, w_out_a, conv_w_b, conv_b_b, ln_g_b, ln_b_b, w_in_c, w_out_c,
           lower_bounds, gnorm_c, w_mem_q, w_mem_kv, w_mem_o, w_ffn_in, ffn_conv_w, ffn_conv_b,
           w_ffn_out):
    bp, tp, d = x_prompt.shape
    bs, ts, _ = x_sample.shape
    depth = g_mix.shape[0]
    n_mem = mem_prompt.shape[1]
    n_pages = page_table.shape[1]
    dff = w_ffn_out.shape[1]
    b_ch = conv_w_b.shape[2]
    c_heads = w_out_c.shape[1] // C_KEY_DIM
    mw = w_mem_q.shape[2]
    mp, ms = bp * tp, bs * ts

    xp = x_prompt.reshape(mp, d)
    xs = x_sample.reshape(ms, d)
    pt_flat = page_table.reshape(-1)
    mem2 = mem_prompt.reshape(bp * n_mem, d)
    cmk = cache_mem_k.reshape(depth, bs, n_mem, mw)
    cmv = cache_mem_v.reshape(depth, bs, n_mem, mw)
    w_ffn_out_bf = w_ffn_out.astype(BF16)

    outs = {k: [] for k in ("mk_p", "mv_p", "mk_s", "mv_s", "cv_p", "cv_s", "hg_p", "hg_s",
                            "memk", "memv", "ff_p", "ff_s")}
    for l in range(depth):
        j = l // 2
        hp = _rmsnorm(xp, g_mix[l], BF16)
        hs = _rmsnorm(xs, g_mix[l], F32)
        if l % 2 == 0:
            qp, qs = _proj([hp], hs, w_in_a, j, n_off=0, n=A_WIDTH, name="in_a_q")
            kp, ks = _proj([hp], hs, w_in_a, j, n_off=A_WIDTH, n=A_WIDTH, name="in_a_k")
            vp, vs = _proj([hp], hs, w_in_a, j, n_off=2 * A_WIDTH, n=A_WIDTH, name="in_a_v")
            glu_p, glu_s = _proj([hp], hs, w_in_a, j, n_off=3 * A_WIDTH, n=2 * b_ch, name="in_a_glu")
            qs3 = qs.reshape(bs, ts, A_WIDTH)
            oa_p = _moba_prompt(qp, kp, vp, bp, tp)
            c_p, buf_p = _convmod_prompt(glu_p, bp, tp, conv_w_b[j], conv_b_b[j],
                                         ln_g_b[j], ln_b_b[j])
            kmean = _moba_kmean(cache_moba_k, pt_flat, j, bs, n_pages)
            sel = _moba_select(qs3, kmean.reshape(bs, -1, A_WIDTH))
            oa_s = _moba_sample(qs3, ks.reshape(bs, ts, A_WIDTH), vs.reshape(bs, ts, A_WIDTH),
                                cache_moba_k, cache_moba_v, sel.reshape(-1), pt_flat, j, n_pages)
            c_s, buf_s = _convmod_sample(glu_s.reshape(bs, ts, 2 * b_ch), cache_convmod[j],
                                         conv_w_b[j], conv_b_b[j], ln_g_b[j], ln_b_b[j])
            cat_s = jnp.concatenate([oa_s.reshape(ms, A_WIDTH), c_s.reshape(ms, b_ch)], axis=1)
            xp, xs = _proj([oa_p, c_p], cat_s, w_out_a, j, res=(xp, xs), name="out_a")
            outs["mk_p"].append(kp.reshape(bp, tp, A_HEADS, A_HEAD_DIM))
            outs["mv_p"].append(vp.reshape(bp, tp, A_HEADS, A_HEAD_DIM))
            outs["mk_s"].append(ks.reshape(bs, ts, A_HEADS, A_HEAD_DIM))
            outs["mv_s"].append(vs.reshape(bs, ts, A_HEADS, A_HEAD_DIM))
            outs["cv_p"].append(buf_p)
            outs["cv_s"].append(buf_s)
        else:
            zp, zs = _proj([hp], hs, w_in_c, j, name="in_c")
            og_p, st_p = _hgrn_prompt(zp, bp, tp, c_heads, lower_bounds, gnorm_c[j], l)
            og_s, st_s = _hgrn_sample(zs.reshape(bs, ts, -1), c_heads, lower_bounds, gnorm_c[j],
                                      state_hgrn, j)
            xp, xs = _proj([og_p], og_s.reshape(ms, -1), w_out_c, j, res=(xp, xs), name="out_c")
            outs["hg_p"].append(st_p)
            outs["hg_s"].append(st_s)

        hm = _rmsnorm(mem2, g_mem_kv[l], BF16)
        mk = _mm(hm, w_mem_kv, l, tm=bp * n_mem, tn=PROJ_TN, n_off=0, n=mw, name="mem_k")
        mv = _mm(hm, w_mem_kv, l, tm=bp * n_mem, tn=PROJ_TN, n_off=mw, n=mw, name="mem_v")
        hp = _rmsnorm(xp, g_mem_q[l], BF16)
        hs = _rmsnorm(xs, g_mem_q[l], F32)
        qp, qs = _proj([hp], hs, w_mem_q, l, out_dtype=BF16, name="mem_q")
        ap = _mem_attn_prompt(qp, mk, mv, bp, tp, n_mem)
        a_s = _mem_attn_sample(qs.reshape(bs, ts, mw), cmk, cmv, l)
        xp, xs = _proj([ap], a_s.reshape(ms, mw), w_mem_o, l, res=(xp, xs), name="mem_o")
        outs["memk"].append(mk.reshape(bp, n_mem, MEM_HEADS, mw // MEM_HEADS))
        outs["memv"].append(mv.reshape(bp, n_mem, MEM_HEADS, mw // MEM_HEADS))

        hp = _rmsnorm(xp, g_ffn[l], BF16)
        hs = _rmsnorm(xs, g_ffn[l], F32)
        hid_p, fb_p, gs, us = _ffn_in(hp, hs, w_ffn_in, l, ffn_conv_w[l], ffn_conv_b[l], tp, dff)
        hid_s, fb_s = _ffn_act_sample(gs.reshape(bs, ts, dff), us.reshape(bs, ts, dff),
                                      cache_ffn_conv, l, ffn_conv_w[l], ffn_conv_b[l])
        xp = _mm(hid_p, w_ffn_out_bf, l, tm=PROMPT_TM, tn=PROJ_TN, tk=dff // 2, res=xp,
                 name="ffn_out_prompt")
        xs = _mm(hid_s.reshape(ms, dff), w_ffn_out_bf, l, tm=ms, tn=PROJ_TN, tk=dff // 2, res=xs,
                 name="ffn_out_sample")
        outs["ff_p"].append(fb_p)
        outs["ff_s"].append(fb_s)

    y_prompt = _rmsnorm(xp, g_final, F32).reshape(bp, tp, d)
    y_sample = _rmsnorm(xs, g_final, F32).reshape(bs, ts, d)
    st = jnp.stack
    return (y_prompt, y_sample, st(outs["mk_p"]), st(outs["mv_p"]), st(outs["mk_s"]), st(outs["mv_s"]),
            st(outs["cv_p"]), st(outs["cv_s"]), st(outs["hg_p"]), st(outs["hg_s"]),
            st(outs["memk"]), st(outs["memv"]), st(outs["ff_p"]), st(outs["ff_s"]))
```

```python
import functools
import math

import jax
import jax.numpy as jnp
from jax import lax
from jax.experimental import pallas as pl
from jax.experimental.pallas import tpu as pltpu

F32 = jnp.float32
BF16 = jnp.bfloat16

EPS = 1e-6
NEG = -1e30

PAGE_SIZE = 128
A_HEADS = 16
A_HEAD_DIM = 128
A_WIDTH = A_HEADS * A_HEAD_DIM
MOBA_BLOCK = 256
MOBA_TOPK = 3
B_CONV_WIDTH = 31
C_KEY_DIM = 128
MEM_HEADS = 4
FFN_CONV_WIDTH = 3

LANES = 128
F32_SUBLANES = 8
BF16_SUBLANES = 16
VMEM_LIMIT = 60 * 1024 * 1024

NT_DIMS = (((1,), (1,)), ((), ()))
TN_DIMS = (((0,), (0,)), ((), ()))


def _params(sem, vmem=VMEM_LIMIT):
    return pltpu.CompilerParams(dimension_semantics=sem, vmem_limit_bytes=vmem)


def _sigmoid(x):
    return 1.0 / (1.0 + jnp.exp(-x))


def _split3(a):
    a1 = a.astype(BF16)
    r1 = a - a1.astype(F32)
    a2 = r1.astype(BF16)
    a3 = (r1 - a2.astype(F32)).astype(BF16)
    return a1, a2, a3


def _dot_f32(a, b, dims):
    a1, a2, a3 = _split3(a)
    b1, b2, b3 = _split3(b)
    d = lambda x, y: lax.dot_general(x, y, dims, preferred_element_type=F32)
    low = d(a1, b3) + d(a3, b1) + d(a2, b2)
    mid = d(a1, b2) + d(a2, b1)
    return (low + mid) + d(a1, b1)


def _rmsnorm_kernel(x_ref, g_ref, o_ref):
    x = x_ref[...]
    ms = jnp.mean(x * x, axis=-1, keepdims=True)
    o_ref[...] = (x * lax.rsqrt(ms + EPS) * g_ref[...]).astype(o_ref.dtype)


def _rmsnorm(x, g, out_dtype):
    m, d = x.shape
    tm = min(m, 512)
    return pl.pallas_call(
        _rmsnorm_kernel,
        grid=(m // tm,),
        in_specs=[pl.BlockSpec((tm, d), lambda i: (i, 0)),
                  pl.BlockSpec((1, d), lambda i: (0, 0))],
        out_specs=pl.BlockSpec((tm, d), lambda i: (i, 0)),
        out_shape=jax.ShapeDtypeStruct((m, d), out_dtype),
        compiler_params=_params(("parallel",)),
        name="rmsnorm",
    )(x, g.reshape(1, d))


PROJ_TM = 2048
PROJ_TN = 256
PROJ_TK = 1024
MM_TM = 1024
MM_TN = 512


def _dot_cast_chunks(x_ref, w_ref, k0, kw):
    acc = None
    for c0 in range(0, kw, PROJ_TK):
        c1 = min(c0 + PROJ_TK, kw)
        wb = w_ref[0, k0 + c0:k0 + c1, :].astype(BF16)
        part = jnp.dot(x_ref[:, c0:c1], wb, preferred_element_type=F32)
        acc = part if acc is None else acc + part
    return acc


def _sample_rows(store, compute):
    @pl.when(pl.program_id(0) == 0)
    def _():
        store(compute())

    @pl.when(pl.program_id(0) != 0)
    def _():
        store(None)


def _proj_kernel(*refs, k_parts, has_res):
    nparts = len(k_parts)
    xp_refs = refs[:nparts]
    if has_res:
        xs_ref, w_ref, rp_ref, rs_ref, yp_ref, ys_ref = refs[nparts:]
    else:
        xs_ref, w_ref, yp_ref, ys_ref = refs[nparts:]

    acc = None
    k0 = 0
    for xp_ref, kw in zip(xp_refs, k_parts):
        part = _dot_cast_chunks(xp_ref, w_ref, k0, kw)
        acc = part if acc is None else acc + part
        k0 += kw
    if has_res:
        acc = acc + rp_ref[...]
    yp_ref[...] = acc.astype(yp_ref.dtype)

    def sample():
        ys = jnp.dot(xs_ref[...].astype(BF16), w_ref[0].astype(BF16), preferred_element_type=F32)
        return ys + rs_ref[...] if has_res else ys

    def store(ys):
        ys_ref[0] = jnp.zeros(ys_ref.shape[1:], F32) if ys is None else ys

    _sample_rows(store, sample)


def _proj(xp_parts, xs, w, layer, *, n_off=0, n=None, res=None, out_dtype=F32, name="proj"):
    mp = xp_parts[0].shape[0]
    k_parts = tuple(x.shape[1] for x in xp_parts)
    kdim = sum(k_parts)
    ms = xs.shape[0]
    n = w.shape[2] if n is None else n
    tm, tn = PROJ_TM, PROJ_TN
    assert mp % tm == 0 and n % tn == 0 and n_off % tn == 0 and w.shape[1] == kdim
    assert xs.shape[1] == kdim
    joff = n_off // tn
    in_specs = [pl.BlockSpec((tm, kw), lambda i, j: (i, 0)) for kw in k_parts]
    in_specs += [pl.BlockSpec((ms, kdim), lambda i, j: (0, 0)),
                 pl.BlockSpec((1, kdim, tn), lambda i, j: (layer, 0, j + joff))]
    args = list(xp_parts) + [xs, w]
    if res is not None:
        in_specs += [pl.BlockSpec((tm, tn), lambda i, j: (i, j)),
                     pl.BlockSpec((ms, tn), lambda i, j: (0, j))]
        args += list(res)
    yp, ys = pl.pallas_call(
        functools.partial(_proj_kernel, k_parts=k_parts, has_res=res is not None),
        grid=(mp // tm, n // tn),
        in_specs=in_specs,
        out_specs=[pl.BlockSpec((tm, tn), lambda i, j: (i, j)),
                   pl.BlockSpec((1, ms, tn), lambda i, j: (i, 0, j))],
        out_shape=[jax.ShapeDtypeStruct((mp, n), out_dtype),
                   jax.ShapeDtypeStruct((mp // tm, ms, n), F32)],
        compiler_params=_params(("parallel", "parallel")),
        name=name,
    )(*args)
    return yp, ys[0]


def _mm_kernel(x_ref, w_ref, *rest, nk, has_res):
    if has_res:
        r_ref, o_ref = rest
    else:
        (o_ref,) = rest
    acc = jnp.dot(x_ref[...].astype(BF16), w_ref[0].astype(BF16), preferred_element_type=F32)
    if nk == 1:
        if has_res:
            acc = acc + r_ref[...]
        o_ref[...] = acc.astype(o_ref.dtype)
    else:
        k = pl.program_id(2)

        @pl.when(k == 0)
        def _():
            o_ref[...] = (acc + r_ref[...]) if has_res else acc

        @pl.when(k > 0)
        def _():
            o_ref[...] += acc


def _mm(x, w, layer, *, tm, tn, tk=None, n_off=0, n=None, res=None, out_dtype=F32, name="matmul"):
    m, kdim = x.shape
    n = w.shape[2] if n is None else n
    tk = kdim if tk is None else tk
    nk = kdim // tk
    assert m % tm == 0 and n % tn == 0 and kdim % tk == 0 and n_off % tn == 0
    assert nk == 1 or out_dtype == F32
    joff = n_off // tn
    in_specs = [pl.BlockSpec((tm, tk), lambda i, j, k: (i, k)),
                pl.BlockSpec((1, tk, tn), lambda i, j, k: (layer, k, j + joff))]
    args = [x, w]
    if res is not None:
        in_specs.append(pl.BlockSpec((tm, tn), lambda i, j, k: (i, j)))
        args.append(res)
    return pl.pallas_call(
        functools.partial(_mm_kernel, nk=nk, has_res=res is not None),
        grid=(m // tm, n // tn, nk),
        in_specs=in_specs,
        out_specs=pl.BlockSpec((tm, tn), lambda i, j, k: (i, j)),
        out_shape=jax.ShapeDtypeStruct((m, n), out_dtype),
        compiler_params=_params(("parallel", "parallel", "arbitrary")),
        name=name,
    )(*args)


def _moba_prompt_kernel(q_ref, k_ref, v_ref, o_ref, *, nblk):
    blk = MOBA_BLOCK
    scale = 1.0 / math.sqrt(A_HEAD_DIM)
    k = k_ref[...]
    kb = k.astype(BF16)
    vb = v_ref[...].astype(BF16)
    kmean = jnp.mean(k.reshape(nblk, blk, A_HEAD_DIM), axis=1)
    krow = lax.broadcasted_iota(jnp.int32, (blk, blk), 0)
    qcol = lax.broadcasted_iota(jnp.int32, (blk, blk), 1)
    causal = krow <= qcol
    for qb in range(nblk):
        qf = q_ref[qb * blk:(qb + 1) * blk, :]
        qbf = (qf * scale).astype(BF16)
        sel = None
        if qb > MOBA_TOPK:
            gate = _dot_f32(kmean, qf, NT_DIMS)
            rows = [gate[n:n + 1, :] for n in range(qb)]
            sel = []
            for n in range(qb):
                rank = jnp.zeros((1, blk), jnp.int32)
                for n2 in range(qb):
                    if n2 == n:
                        continue
                    beats = (rows[n2] >= rows[n]) if n2 < n else (rows[n2] > rows[n])
                    rank = rank + beats.astype(jnp.int32)
                sel.append(rank < MOBA_TOPK)
        s_list = []
        for n in range(qb + 1):
            s = lax.dot_general(kb[n * blk:(n + 1) * blk], qbf, NT_DIMS,
                                preferred_element_type=F32)
            if n == qb:
                s = jnp.where(causal, s, NEG)
            elif sel is not None:
                s = jnp.where(sel[n], s, NEG)
            s_list.append(s)
        m = s_list[0].max(axis=0, keepdims=True)
        for s in s_list[1:]:
            m = jnp.maximum(m, s.max(axis=0, keepdims=True))
        l = jnp.zeros((1, blk), F32)
        ot = jnp.zeros((A_HEAD_DIM, blk), F32)
        for n, s in enumerate(s_list):
            p = jnp.exp(s - m)
            l = l + p.sum(axis=0, keepdims=True)
            ot = ot + lax.dot_general(vb[n * blk:(n + 1) * blk], p.astype(BF16), TN_DIMS,
                                      preferred_element_type=F32)
        o_ref[qb * blk:(qb + 1) * blk, :] = (ot / l).T.astype(o_ref.dtype)


def _moba_prompt(q, k, v, nb, t):
    hd = A_HEAD_DIM
    spec = lambda: pl.BlockSpec((t, hd), lambda b, h: (b, h))
    return pl.pallas_call(
        functools.partial(_moba_prompt_kernel, nblk=t // MOBA_BLOCK),
        grid=(nb, A_HEADS),
        in_specs=[spec(), spec(), spec()],
        out_specs=spec(),
        out_shape=jax.ShapeDtypeStruct((nb * t, A_WIDTH), BF16),
        compiler_params=_params(("parallel", "parallel")),
        name="moba_prompt",
    )(q, k, v)


PAGES_PER_BLOCK = MOBA_BLOCK // PAGE_SIZE
KMEAN_BLOCKS = 2


def _kmean_kernel(pt_ref, *refs):
    o_ref = refs[-1]
    for i in range(KMEAN_BLOCKS):
        s = jnp.zeros((A_HEADS, A_HEAD_DIM), F32)
        for pg in range(PAGES_PER_BLOCK):
            s = s + jnp.sum(refs[i * PAGES_PER_BLOCK + pg][0, 0], axis=0)
        o_ref[0, i] = s * (1.0 / MOBA_BLOCK)


def _moba_kmean(cache_k, pt_flat, j, nb, n_pages):
    nblk = n_pages // PAGES_PER_BLOCK
    per_step = KMEAN_BLOCKS * PAGES_PER_BLOCK
    page = lambda pg: pl.BlockSpec(
        (1, 1, PAGE_SIZE, A_HEADS, A_HEAD_DIM),
        lambda b, n, pt: (j, pt[b * n_pages + per_step * n + pg], 0, 0, 0))
    return pl.pallas_call(
        _kmean_kernel,
        grid_spec=pltpu.PrefetchScalarGridSpec(
            num_scalar_prefetch=1,
            grid=(nb, nblk // KMEAN_BLOCKS),
            in_specs=[page(pg) for pg in range(per_step)],
            out_specs=pl.BlockSpec((1, KMEAN_BLOCKS, A_HEADS, A_HEAD_DIM),
                                   lambda b, n, pt: (b, n, 0, 0)),
        ),
        out_shape=jax.ShapeDtypeStruct((nb, nblk, A_HEADS, A_HEAD_DIM), F32),
        compiler_params=_params(("parallel", "parallel")),
        name="moba_kmean",
    )(pt_flat, *([cache_k] * per_step))


def _moba_select_kernel(q_ref, km_ref, o_ref, *, ts, nblk):
    q = q_ref[0]
    km = km_ref[0]
    nq = ts * A_HEADS
    qrep = jnp.concatenate(
        [jnp.broadcast_to(q[t:t + 1], (A_HEADS, A_WIDTH)) for t in range(ts)], axis=0)
    rowh = lax.broadcasted_iota(jnp.int32, (nq, A_WIDTH), 0) % A_HEADS
    colh = lax.broadcasted_iota(jnp.int32, (nq, A_WIDTH), 1) // A_HEAD_DIM
    qbig = jnp.where(rowh == colh, qrep, 0.0)
    gate = _dot_f32(km, qbig, NT_DIMS)
    nidx = lax.broadcasted_iota(jnp.int32, (nblk, nq), 0)
    for r in range(MOBA_TOPK):
        m = gate.max(axis=0, keepdims=True)
        idx = jnp.min(jnp.where(gate == m, nidx, nblk), axis=0, keepdims=True)
        o_ref[0, r:r + 1, :] = idx
        gate = jnp.where(nidx == idx, -jnp.inf, gate)


def _moba_select(qs3, kmean):
    nb, ts, _ = qs3.shape
    nblk = kmean.shape[1]
    nq = ts * A_HEADS
    return pl.pallas_call(
        functools.partial(_moba_select_kernel, ts=ts, nblk=nblk),
        grid=(nb,),
        in_specs=[pl.BlockSpec((1, ts, A_WIDTH), lambda b: (b, 0, 0)),
                  pl.BlockSpec((1, nblk, A_WIDTH), lambda b: (b, 0, 0))],
        out_specs=pl.BlockSpec((1, MOBA_TOPK, nq), lambda b: (b, 0, 0)),
        out_shape=jax.ShapeDtypeStruct((nb, MOBA_TOPK, nq), jnp.int32),
        compiler_params=_params(("parallel",)),
        name="moba_select",
    )(qs3, kmean)


def _moba_sample_kernel(sel_ref, pt_ref, q_ref, kn_ref, vn_ref, ck_hbm, cv_hbm, o_ref,
                        kbuf, vbuf, sems, *, j, ts, n_pages):
    b = pl.program_id(0)
    h = pl.program_id(1)
    nblk = n_pages // PAGES_PER_BLOCK
    nq = ts * A_HEADS
    step = b * A_HEADS + h
    nsteps = pl.num_programs(0) * A_HEADS
    half = step % 2

    def page_copies(bi, hi, hf):
        copies = []
        for t in range(ts):
            for r in range(MOBA_TOPK):
                blkid = jnp.minimum(sel_ref[bi * (MOBA_TOPK * nq) + r * nq + t * A_HEADS + hi], nblk - 1)
                for pg in range(PAGES_PER_BLOCK):
                    page = pt_ref[bi * n_pages + blkid * PAGES_PER_BLOCK + pg]
                    slot = (t * MOBA_TOPK + r) * PAGES_PER_BLOCK + pg
                    copies.append(pltpu.make_async_copy(
                        ck_hbm.at[j, page, :, hi, :], kbuf.at[hf, slot], sems.at[hf]))
                    copies.append(pltpu.make_async_copy(
                        cv_hbm.at[j, page, :, hi, :], vbuf.at[hf, slot], sems.at[hf]))
        return copies

    @pl.when(step == 0)
    def _():
        for c in page_copies(b, h, half):
            c.start()

    @pl.when(step + 1 < nsteps)
    def _():
        nxt = step + 1
        for c in page_copies(nxt // A_HEADS, nxt % A_HEADS, 1 - half):
            c.start()

    for c in page_copies(b, h, half):
        c.wait()

    scale = 1.0 / math.sqrt(A_HEAD_DIM)
    npg = MOBA_TOPK * PAGES_PER_BLOCK
    trow = lax.broadcasted_iota(jnp.int32, (ts, 1), 0)
    for t in range(ts):
        q = q_ref[0, t:t + 1, :] * scale
        slots = range(t * npg, (t + 1) * npg)
        s_cols = [jnp.sum(kbuf[half, sl] * q, axis=1, keepdims=True) for sl in slots]
        s_own = jnp.sum(kn_ref[0] * q, axis=1, keepdims=True)
        s_own = jnp.where(trow <= t, s_own, NEG)
        m = s_own.max(axis=0, keepdims=True)
        for s in s_cols:
            m = jnp.maximum(m, s.max(axis=0, keepdims=True))
        p_own = jnp.exp(s_own - m)
        l = p_own.sum(axis=0, keepdims=True)
        o = jnp.sum(p_own * vn_ref[0], axis=0, keepdims=True)
        for s, sl in zip(s_cols, slots):
            p = jnp.exp(s - m)
            l = l + p.sum(axis=0, keepdims=True)
            o = o + jnp.sum(p * vbuf[half, sl], axis=0, keepdims=True)
        o_ref[0, t:t + 1, :] = o / l


def _moba_sample(qs3, ks3, vs3, cache_k, cache_v, sel_flat, pt_flat, j, n_pages):
    nb, ts, _ = qs3.shape
    hd = A_HEAD_DIM
    nslot = ts * MOBA_TOPK * PAGES_PER_BLOCK
    own = lambda: pl.BlockSpec((1, ts, hd), lambda b, h, sel, pt: (b, 0, h))
    hbm = lambda: pl.BlockSpec(memory_space=pl.ANY)
    return pl.pallas_call(
        functools.partial(_moba_sample_kernel, j=j, ts=ts, n_pages=n_pages),
        grid_spec=pltpu.PrefetchScalarGridSpec(
            num_scalar_prefetch=2,
            grid=(nb, A_HEADS),
            in_specs=[own(), own(), own(), hbm(), hbm()],
            out_specs=own(),
            scratch_shapes=[pltpu.VMEM((2, nslot, PAGE_SIZE, hd), F32),
                            pltpu.VMEM((2, nslot, PAGE_SIZE, hd), F32),
                            pltpu.SemaphoreType.DMA((2,))],
        ),
        out_shape=jax.ShapeDtypeStruct((nb, ts, A_WIDTH), F32),
        compiler_params=_params(("arbitrary", "arbitrary")),
        name="moba_sample",
    )(sel_flat, pt_flat, qs3, ks3, vs3, cache_k, cache_v)


def _layernorm_swish(y, g, b):
    mu = jnp.mean(y, axis=-1, keepdims=True)
    yc = y - mu
    var = jnp.mean(yc * yc, axis=-1, keepdims=True)
    yn = yc * lax.rsqrt(var + EPS) * g + b
    return yn * _sigmoid(yn)


CONV_HALO = 32
CONV_ROWS = 128
CONV_COLS = 128


def _convmod_prompt_kernel(ga_ref, gb_ref, cw_ref, cb_ref, lg_ref, lb_ref, c_ref, nb_ref,
                           ubuf, ybuf, wbuf, *, tt, nt):
    w = B_CONV_WIDTH
    ch = ubuf.shape[1]
    t = pl.program_id(1)

    @pl.when(t == 0)
    def _():
        ubuf[0:CONV_HALO, :] = jnp.zeros((CONV_HALO, ch), F32)

    ubuf[CONV_HALO:CONV_HALO + tt, :] = ga_ref[...] * _sigmoid(gb_ref[...])
    base = CONV_HALO - (w - 1)
    for c0 in range(0, ch, CONV_COLS):
        cols = slice(c0, c0 + CONV_COLS)
        for r0 in range(0, tt, CONV_ROWS):
            acc = jnp.broadcast_to(cb_ref[:, cols], (CONV_ROWS, CONV_COLS))
            for s in range(F32_SUBLANES):
                taps = range(s, w, F32_SUBLANES)
                start = r0 + base + s
                rows = CONV_ROWS + F32_SUBLANES * (len(taps) - 1)
                wbuf[s, 0:rows, :] = ubuf[start:start + rows, cols]
                for a, j in enumerate(taps):
                    acc = acc + cw_ref[j:j + 1, cols] * \
                        wbuf[s, a * F32_SUBLANES:a * F32_SUBLANES + CONV_ROWS, :]
            ybuf[r0:r0 + CONV_ROWS, cols] = acc
    c_ref[...] = _layernorm_swish(ybuf[...], lg_ref[...], lb_ref[...]).astype(c_ref.dtype)

    @pl.when(t == nt - 1)
    def _():
        nb_ref[0] = ubuf[CONV_HALO + tt - (w - 1):CONV_HALO + tt, :]

    ubuf[0:CONV_HALO, :] = ubuf[tt:tt + CONV_HALO, :]


def _convmod_prompt(glu, nb, t, cw, cb, lg, lb):
    ch = glu.shape[1] // 2
    tt = 256
    nt = t // tt
    vec = lambda: pl.BlockSpec((1, ch), lambda b, i: (0, 0))
    return pl.pallas_call(
        functools.partial(_convmod_prompt_kernel, tt=tt, nt=nt),
        grid=(nb, nt),
        in_specs=[pl.BlockSpec((tt, ch), lambda b, i: (b * nt + i, 0)),
                  pl.BlockSpec((tt, ch), lambda b, i: (b * nt + i, 1)),
                  pl.BlockSpec((B_CONV_WIDTH, ch), lambda b, i: (0, 0)),
                  vec(), vec(), vec()],
        out_specs=[pl.BlockSpec((tt, ch), lambda b, i: (b * nt + i, 0)),
                   pl.BlockSpec((1, B_CONV_WIDTH - 1, ch), lambda b, i: (b, 0, 0))],
        out_shape=[jax.ShapeDtypeStruct((nb * t, ch), BF16),
                   jax.ShapeDtypeStruct((nb, B_CONV_WIDTH - 1, ch), F32)],
        scratch_shapes=[pltpu.VMEM((CONV_HALO + tt, ch), F32), pltpu.VMEM((tt, ch), F32),
                        pltpu.VMEM((F32_SUBLANES, CONV_ROWS + CONV_HALO, CONV_COLS), F32)],
        compiler_params=_params(("parallel", "arbitrary")),
        name="convmod_prompt",
    )(glu, glu, cw, cb.reshape(1, ch), lg.reshape(1, ch), lb.reshape(1, ch))


def _convmod_sample_kernel(ga_ref, gb_ref, cache_ref, cw_ref, cb_ref, lg_ref, lb_ref,
                           c_ref, nb_ref, ubuf, *, ts):
    w = B_CONV_WIDTH
    ch = ubuf.shape[1]
    ubuf[0:w - 1, :] = cache_ref[0]
    ubuf[w - 1:w - 1 + ts, :] = ga_ref[0] * _sigmoid(gb_ref[0])
    acc = jnp.broadcast_to(cb_ref[...], (ts, ch))
    for j in range(w):
        acc = acc + cw_ref[j:j + 1, :] * ubuf[j:j + ts, :]
    c_ref[0] = _layernorm_swish(acc, lg_ref[...], lb_ref[...])
    nb_ref[0] = ubuf[ts:ts + w - 1, :]


def _convmod_sample(glu3, cache, cw, cb, lg, lb):
    nb, ts, ch2 = glu3.shape
    ch = ch2 // 2
    w = B_CONV_WIDTH
    vec = lambda: pl.BlockSpec((1, ch), lambda b: (0, 0))
    return pl.pallas_call(
        functools.partial(_convmod_sample_kernel, ts=ts),
        grid=(nb,),
        in_specs=[pl.BlockSpec((1, ts, ch), lambda b: (b, 0, 0)),
                  pl.BlockSpec((1, ts, ch), lambda b: (b, 0, 1)),
                  pl.BlockSpec((1, w - 1, ch), lambda b: (b, 0, 0)),
                  pl.BlockSpec((w, ch), lambda b: (0, 0)),
                  vec(), vec(), vec()],
        out_specs=[pl.BlockSpec((1, ts, ch), lambda b: (b, 0, 0)),
                   pl.BlockSpec((1, w - 1, ch), lambda b: (b, 0, 0))],
        out_shape=[jax.ShapeDtypeStruct((nb, ts, ch), F32),
                   jax.ShapeDtypeStruct((nb, w - 1, ch), F32)],
        scratch_shapes=[pltpu.VMEM((w - 1 + ts + 5, ch), F32)],
        compiler_params=_params(("parallel",)),
        name="convmod_sample",
    )(glu3, glu3, cache, cw, cb.reshape(1, ch), lg.reshape(1, ch), lb.reshape(1, ch))


HG_CHUNK = 128
HG_SUB = 16
HG_HEADS = 4


def _hgrn_lower_bound(lbp, layer):
    e = jnp.exp(lbp - lbp.max(axis=0, keepdims=True))
    sm = e / e.sum(axis=0, keepdims=True)
    lb = jnp.zeros_like(sm[0:1])
    for i in range(1, layer + 1):
        lb = lb + sm[i:i + 1]
    return lb


def _hgrn_out(o, g, gn):
    ms = jnp.mean(o * o, axis=-1, keepdims=True)
    return (o * lax.rsqrt(ms + EPS) * gn) * (g * _sigmoid(g))


def _hgrn_chunk(q, f, v, g, lb, gn, s_prev, kpad, bpad, consts):
    cs, sub, kd = HG_CHUNK, HG_SUB, C_KEY_DIM
    tril, rowk, eye, lane, blockdiag = consts
    fg = lb + (1.0 - lb) * _sigmoid(f)
    kk = 1.0 - fg
    vb = v.astype(BF16)
    l1, l2, l3 = _split3(jnp.log(fg))
    cum = lambda x: jnp.dot(tril, x, preferred_element_type=F32)
    b = (cum(l3) + cum(l2)) + cum(l1)

    o = jnp.dot((q * jnp.exp(b)).astype(BF16), s_prev.astype(BF16), preferred_element_type=F32)

    bref_rows = jnp.concatenate(
        [jnp.zeros((sub, kd), F32)]
        + [jnp.broadcast_to(b[sub * i - 1:sub * i], (sub, kd)) for i in range(1, cs // sub)], axis=0)
    qe = (q * jnp.exp(b - bref_rows)).astype(BF16)
    parts = [jnp.zeros((sub, cs), F32)]
    for i in range(1, cs // sub):
        bref = b[sub * i - 1:sub * i]
        ke = jnp.where(rowk < sub * i, kk * jnp.exp(jnp.minimum(bref - b, 0.0)), 0.0).astype(BF16)
        parts.append(lax.dot_general(qe[sub * i:sub * (i + 1)], ke, NT_DIMS,
                                     preferred_element_type=F32))
    a = jnp.concatenate(parts, axis=0)

    kpad[sub:, :] = kk
    bpad[sub:, :] = b
    band = jnp.zeros((cs, cs), F32)
    for d in range(sub):
        if d == 0:
            p = q * kk
        else:
            p = q * kpad[sub - d:sub - d + cs, :] * jnp.exp(b - bpad[sub - d:sub - d + cs, :])
        band = jnp.where(lane == (cs - d) % cs, p.sum(axis=1, keepdims=True), band)
    band = pltpu.roll(band, 0, 1, stride=1, stride_axis=0)
    a = a + jnp.where(blockdiag, band, 0.0)
    o = o + jnp.dot(a.astype(BF16), vb, preferred_element_type=F32)

    bl = b[cs - 1:cs]
    kd_mat = (kk * jnp.exp(bl - b)).astype(BF16)
    decay_col = jnp.sum(jnp.where(eye, jnp.broadcast_to(jnp.exp(bl), (kd, kd)), 0.0),
                        axis=1, keepdims=True)
    s_new = decay_col * s_prev + lax.dot_general(kd_mat, vb, TN_DIMS, preferred_element_type=F32)
    return _hgrn_out(o, g, gn), s_new


def _hgrn_prompt_kernel(q_ref, f_ref, i_ref, g_ref, lbp_ref, gn_ref, o_ref, s_ref,
                        kpad, bpad, *, layer):
    cs, sub, kd = HG_CHUNK, HG_SUB, C_KEY_DIM

    @pl.when(pl.program_id(2) == 0)
    def _():
        s_ref[...] = jnp.zeros_like(s_ref)

    rr = lax.broadcasted_iota(jnp.int32, (cs, cs), 0)
    cc = lax.broadcasted_iota(jnp.int32, (cs, cs), 1)
    consts = (jnp.where(cc <= rr, 1.0, 0.0).astype(BF16),
              lax.broadcasted_iota(jnp.int32, (cs, kd), 0),
              rr == cc,
              cc,
              (rr // sub == cc // sub) & (cc <= rr))
    kpad[:, 0:sub, :] = jnp.zeros((HG_HEADS, sub, kd), F32)
    bpad[:, 0:sub, :] = jnp.zeros((HG_HEADS, sub, kd), F32)
    lbs = _hgrn_lower_bound(lbp_ref[...], layer)
    for h in range(HG_HEADS):
        cols = slice(h * kd, (h + 1) * kd)
        og, s_new = _hgrn_chunk(q_ref[:, cols], f_ref[:, cols], i_ref[:, cols], g_ref[:, cols],
                                lbs[:, cols], gn_ref[...], s_ref[0, h],
                                kpad.at[h], bpad.at[h], consts)
        s_ref[0, h] = s_new
        o_ref[:, cols] = og.astype(o_ref.dtype)


def _hgrn_prompt(z, nb, t, heads, lower_bounds, gnorm, layer):
    kd = C_KEY_DIM
    assert HG_CHUNK == kd and heads % HG_HEADS == 0
    nchunk = t // HG_CHUNK
    depth = lower_bounds.shape[0]
    hw = HG_HEADS * kd
    ng = heads // HG_HEADS
    col = lambda part: pl.BlockSpec((HG_CHUNK, hw), lambda b, h, c: (b * nchunk + c, part * ng + h))
    return pl.pallas_call(
        functools.partial(_hgrn_prompt_kernel, layer=layer),
        grid=(nb, ng, nchunk),
        in_specs=[col(0), col(1), col(2), col(3),
                  pl.BlockSpec((depth, hw), lambda b, h, c: (0, h)),
                  pl.BlockSpec((1, kd), lambda b, h, c: (0, 0))],
        out_specs=[pl.BlockSpec((HG_CHUNK, hw), lambda b, h, c: (b * nchunk + c, h)),
                   pl.BlockSpec((1, HG_HEADS, kd, kd), lambda b, h, c: (b, h, 0, 0))],
        out_shape=[jax.ShapeDtypeStruct((nb * t, heads * kd), BF16),
                   jax.ShapeDtypeStruct((nb, heads, kd, kd), F32)],
        scratch_shapes=[pltpu.VMEM((HG_HEADS, HG_SUB + HG_CHUNK, kd), F32),
                        pltpu.VMEM((HG_HEADS, HG_SUB + HG_CHUNK, kd), F32)],
        compiler_params=_params(("parallel", "parallel", "arbitrary")),
        name="hgrn_prompt",
    )(z, z, z, z, lower_bounds, gnorm.reshape(1, kd))


def _hgrn_sample_kernel(q_ref, f_ref, i_ref, g_ref, lbp_ref, gn_ref, s0_ref, o_ref, s_ref,
                        *, layer, ts):
    kd = C_KEY_DIM
    lb = _hgrn_lower_bound(lbp_ref[...], layer)
    fg = lb + (1.0 - lb) * _sigmoid(f_ref[0])
    kk = 1.0 - fg
    q = q_ref[0]
    v = i_ref[0]
    g = g_ref[0]
    eye = lax.broadcasted_iota(jnp.int32, (kd, kd), 0) == lax.broadcasted_iota(jnp.int32, (kd, kd), 1)
    col = lambda x: jnp.sum(jnp.where(eye, jnp.broadcast_to(x, (kd, kd)), 0.0), axis=1, keepdims=True)
    s = s0_ref[0, 0]
    for t in range(ts):
        s = col(fg[t:t + 1]) * s + col(kk[t:t + 1]) * v[t:t + 1]
        o = jnp.sum(col(q[t:t + 1]) * s, axis=0, keepdims=True)
        o_ref[0, t:t + 1, :] = _hgrn_out(o, g[t:t + 1], gn_ref[...])
    s_ref[0, 0] = s


def _hgrn_sample(zs3, heads, lower_bounds, gnorm, state, j):
    nb, ts, _ = zs3.shape
    kd = C_KEY_DIM
    depth = lower_bounds.shape[0]
    st = state.reshape(-1, heads, kd, kd)
    col = lambda part: pl.BlockSpec((1, ts, kd), lambda b, h: (b, 0, part * heads + h))
    return pl.pallas_call(
        functools.partial(_hgrn_sample_kernel, layer=2 * j + 1, ts=ts),
        grid=(nb, heads),
        in_specs=[col(0), col(1), col(2), col(3),
                  pl.BlockSpec((depth, kd), lambda b, h: (0, h)),
                  pl.BlockSpec((1, kd), lambda b, h: (0, 0)),
                  pl.BlockSpec((1, 1, kd, kd), lambda b, h: (j * nb + b, h, 0, 0))],
        out_specs=[pl.BlockSpec((1, ts, kd), lambda b, h: (b, 0, h)),
                   pl.BlockSpec((1, 1, kd, kd), lambda b, h: (b, h, 0, 0))],
        out_shape=[jax.ShapeDtypeStruct((nb, ts, heads * kd), F32),
                   jax.ShapeDtypeStruct((nb, heads, kd, kd), F32)],
        compiler_params=_params(("parallel", "parallel")),
        name="hgrn_sample",
    )(zs3, zs3, zs3, zs3, lower_bounds, gnorm.reshape(1, kd), st)


def _softmax_attend(q, k, v):
    scale = 1.0 / math.sqrt(q.shape[-1])
    s = lax.dot_general(q.astype(BF16), k.astype(BF16), NT_DIMS, preferred_element_type=F32) * scale
    m = s.max(axis=1, keepdims=True)
    p = jnp.exp(s - m)
    l = p.sum(axis=1, keepdims=True)
    return jnp.dot(p.astype(BF16), v.astype(BF16), preferred_element_type=F32) / l


def _mem_attn_prompt_kernel(q_ref, k_ref, v_ref, o_ref):
    o_ref[...] = _softmax_attend(q_ref[...], k_ref[...], v_ref[...]).astype(o_ref.dtype)


def _mem_attn_prompt(q, mk, mv, nb, t, n_mem):
    w = q.shape[1]
    hd = w // MEM_HEADS
    tq = 512
    nt = t // tq
    return pl.pallas_call(
        _mem_attn_prompt_kernel,
        grid=(nb, MEM_HEADS, nt),
        in_specs=[pl.BlockSpec((tq, hd), lambda b, h, i: (b * nt + i, h)),
                  pl.BlockSpec((n_mem, hd), lambda b, h, i: (b, h)),
                  pl.BlockSpec((n_mem, hd), lambda b, h, i: (b, h))],
        out_specs=pl.BlockSpec((tq, hd), lambda b, h, i: (b * nt + i, h)),
        out_shape=jax.ShapeDtypeStruct((nb * t, w), BF16),
        compiler_params=_params(("parallel", "parallel", "parallel")),
        name="mem_attn_prompt",
    )(q, mk, mv)


def _mem_attn_sample_kernel(q_ref, k_ref, v_ref, o_ref):
    o_ref[0] = _softmax_attend(q_ref[0], k_ref[0, 0], v_ref[0, 0])


def _mem_attn_sample(q3, cache_k, cache_v, layer):
    nb, ts, w = q3.shape
    n_mem = cache_k.shape[2]
    hd = w // MEM_HEADS
    kv = lambda: pl.BlockSpec((1, 1, n_mem, hd), lambda b, h: (layer, b, 0, h))
    return pl.pallas_call(
        _mem_attn_sample_kernel,
        grid=(nb, MEM_HEADS),
        in_specs=[pl.BlockSpec((1, ts, hd), lambda b, h: (b, 0, h)), kv(), kv()],
        out_specs=pl.BlockSpec((1, ts, hd), lambda b, h: (b, 0, h)),
        out_shape=jax.ShapeDtypeStruct((nb, ts, w), F32),
        compiler_params=_params(("parallel", "parallel")),
        name="mem_attn_sample",
    )(q3, cache_k, cache_v)


def _ffn_in_kernel(*refs, tm, tiles_per_seq):
    if tiles_per_seq > 1:
        (xp_ref, xprev_ref, xs_ref, wg_ref, wu_ref, cw_ref, cb_ref,
         h_ref, tail_ref, gs_ref, us_ref, gbuf) = refs
    else:
        (xp_ref, xs_ref, wg_ref, wu_ref, cw_ref, cb_ref,
         h_ref, tail_ref, gs_ref, us_ref, gbuf) = refs
    halo = BF16_SUBLANES
    d = xp_ref.shape[1]
    tn = h_ref.shape[1]
    gate = _dot_cast_chunks(xp_ref, wg_ref, 0, d)
    up = _dot_cast_chunks(xp_ref, wu_ref, 0, d)
    if tiles_per_seq > 1:
        gprev = jnp.dot(xprev_ref[...], wg_ref[0].astype(BF16), preferred_element_type=F32)
        gprev = jnp.where(pl.program_id(0) % tiles_per_seq == 0, 0.0, gprev)
    else:
        gprev = jnp.zeros((halo, tn), F32)
    gbuf[0:halo, :] = gprev
    gbuf[halo:halo + tm, :] = gate
    cw = cw_ref[...]
    gc = (cw[0:1] * gbuf[halo - 2:halo - 2 + tm, :] + cw[1:2] * gbuf[halo - 1:halo - 1 + tm, :]
          + cw[2:3] * gate + cb_ref[...])
    h_ref[...] = (gc * _sigmoid(gc) * up).astype(h_ref.dtype)
    tail_ref[0] = gate[tm - (FFN_CONV_WIDTH - 1):tm, :]

    def sample():
        xs = xs_ref[...].astype(BF16)
        return (jnp.dot(xs, wg_ref[0].astype(BF16), preferred_element_type=F32),
                jnp.dot(xs, wu_ref[0].astype(BF16), preferred_element_type=F32))

    def store(gu):
        zero = jnp.zeros(gs_ref.shape[1:], F32)
        gs_ref[0], us_ref[0] = (zero, zero) if gu is None else gu

    _sample_rows(store, sample)


def _ffn_in(hp, hs, w_in, layer, cw, cb, t, dff):
    m, d = hp.shape
    ms = hs.shape[0]
    tm, tn = PROJ_TM, PROJ_TN
    halo = BF16_SUBLANES
    assert t % tm == 0 and dff % tn == 0
    tiles_per_seq = t // tm
    ni, nj = m // tm, dff // tn
    in_specs = [pl.BlockSpec((tm, d), lambda i, j: (i, 0), pipeline_mode=pl.Buffered(1))]
    args = [hp]
    if tiles_per_seq > 1:
        in_specs.append(pl.BlockSpec((halo, d), lambda i, j: (jnp.maximum(i * (tm // halo) - 1, 0), 0)))
        args.append(hp)
    in_specs += [pl.BlockSpec((ms, d), lambda i, j: (0, 0)),
                 pl.BlockSpec((1, d, tn), lambda i, j: (layer, 0, j)),
                 pl.BlockSpec((1, d, tn), lambda i, j: (layer, 0, nj + j)),
                 pl.BlockSpec((FFN_CONV_WIDTH, tn), lambda i, j: (0, j)),
                 pl.BlockSpec((1, tn), lambda i, j: (0, j))]
    args += [hs, w_in, w_in, cw, cb.reshape(1, dff)]
    sample_out = lambda: pl.BlockSpec((1, ms, tn), lambda i, j: (i, 0, j))
    hidden, tails, gs, us = pl.pallas_call(
        functools.partial(_ffn_in_kernel, tm=tm, tiles_per_seq=tiles_per_seq),
        grid=(ni, nj),
        in_specs=in_specs,
        out_specs=[pl.BlockSpec((tm, tn), lambda i, j: (i, j)),
                   pl.BlockSpec((1, FFN_CONV_WIDTH - 1, tn), lambda i, j: (i, 0, j)),
                   sample_out(), sample_out()],
        out_shape=[jax.ShapeDtypeStruct((m, dff), BF16),
                   jax.ShapeDtypeStruct((ni, FFN_CONV_WIDTH - 1, dff), F32),
                   jax.ShapeDtypeStruct((ni, ms, dff), F32),
                   jax.ShapeDtypeStruct((ni, ms, dff), F32)],
        scratch_shapes=[pltpu.VMEM((halo + tm, tn), F32)],
        compiler_params=_params(("parallel", "parallel")),
        name="ffn_in",
    )(*args)
    return hidden, tails[tiles_per_seq - 1::tiles_per_seq], gs[0], us[0]


def _ffn_act_sample_kernel(g_ref, u_ref, cache_ref, cw_ref, cb_ref, h_ref, nb_ref, *, ts):
    w = FFN_CONV_WIDTH
    rows = [cache_ref[0, 0, r:r + 1, :] for r in range(w - 1)] + \
           [g_ref[0, t:t + 1, :] for t in range(ts)]
    for t in range(ts):
        gc = cb_ref[...]
        for j in range(w):
            gc = gc + cw_ref[j:j + 1, :] * rows[t + j]
        h_ref[0, t:t + 1, :] = gc * _sigmoid(gc) * u_ref[0, t:t + 1, :]
    for r in range(w - 1):
        nb_ref[0, r:r + 1, :] = rows[ts + r]


def _ffn_act_sample(gs3, us3, cache, layer, cw, cb):
    nb, ts, dff = gs3.shape
    w = FFN_CONV_WIDTH
    row = lambda: pl.BlockSpec((1, ts, dff), lambda b: (b, 0, 0))
    return pl.pallas_call(
        functools.partial(_ffn_act_sample_kernel, ts=ts),
        grid=(nb,),
        in_specs=[row(), row(),
                  pl.BlockSpec((1, 1, w - 1, dff), lambda b: (layer, b, 0, 0)),
                  pl.BlockSpec((w, dff), lambda b: (0, 0)),
                  pl.BlockSpec((1, dff), lambda b: (0, 0))],
        out_specs=[row(), pl.BlockSpec((1, w - 1, dff), lambda b: (b, 0, 0))],
        out_shape=[jax.ShapeDtypeStruct((nb, ts, dff), F32),
                   jax.ShapeDtypeStruct((nb, w - 1, dff), F32)],
        compiler_params=_params(("parallel",)),
        name="ffn_act_sample",
    )(gs3, us3, cache, cw, cb.reshape(1, dff))


def kernel(x_prompt, x_sample, mem_prompt, cache_moba_k, cache_moba_v, page_table, cache_convmod,
           state_hgrn, cache_mem_k, cache_mem_v, cache_ffn_conv, g_mix, g_mem_q, g_mem_kv, g_ffn,
           g_final, w_in_a, w_out_a, conv_w_b, conv_b_b, ln_g_b, ln_b_b, w_in_c, w_out_c,
           lower_bounds, gnorm_c, w_mem_q, w_mem_kv, w_mem_o, w_ffn_in, ffn_conv_w, ffn_conv_b,
           w_ffn_out):
    bp, tp, d = x_prompt.shape
    bs, ts, _ = x_sample.shape
    depth = g_mix.shape[0]
    n_mem = mem_prompt.shape[1]
    n_pages = page_table.shape[1]
    dff = w_ffn_out.shape[1]
    b_ch = conv_w_b.shape[2]
    c_heads = w_out_c.shape[1] // C_KEY_DIM
    mw = w_mem_q.shape[2]
    mp, ms = bp * tp, bs * ts

    xp = x_prompt.reshape(mp, d)
    xs = x_sample.reshape(ms, d)
    pt_flat = page_table.reshape(-1)
    mem2 = mem_prompt.reshape(bp * n_mem, d)
    cmk = cache_mem_k.reshape(depth, bs, n_mem, mw)
    cmv = cache_mem_v.reshape(depth, bs, n_mem, mw)
    w_ffn_out_bf = w_ffn_out.astype(BF16)

    outs = {k: [] for k in ("mk_p", "mv_p", "mk_s", "mv_s", "cv_p", "cv_s", "hg_p", "hg_s",
                            "memk", "memv", "ff_p", "ff_s")}
    for l in range(depth):
        j = l // 2
        hp = _rmsnorm(xp, g_mix[l], BF16)
        hs = _rmsnorm(xs, g_mix[l], F32)
        if l % 2 == 0:
            qp, qs = _proj([hp], hs, w_in_a, j, n_off=0, n=A_WIDTH, name="in_a_q")
            kp, ks = _proj([hp], hs, w_in_a, j, n_off=A_WIDTH, n=A_WIDTH, name="in_a_k")
            vp, vs = _proj([hp], hs, w_in_a, j, n_off=2 * A_WIDTH, n=A_WIDTH, name="in_a_v")
            glu_p, glu_s = _proj([hp], hs, w_in_a, j, n_off=3 * A_WIDTH, n=2 * b_ch, name="in_a_glu")
            qs3 = qs.reshape(bs, ts, A_WIDTH)
            oa_p = _moba_prompt(qp, kp, vp, bp, tp)
            c_p, buf_p = _convmod_prompt(glu_p, bp, tp, conv_w_b[j], conv_b_b[j],
                                         ln_g_b[j], ln_b_b[j])
            kmean = _moba_kmean(cache_moba_k, pt_flat, j, bs, n_pages)
            sel = _moba_select(qs3, kmean.reshape(bs, -1, A_WIDTH))
            oa_s = _moba_sample(qs3, ks.reshape(bs, ts, A_WIDTH), vs.reshape(bs, ts, A_WIDTH),
                                cache_moba_k, cache_moba_v, sel.reshape(-1), pt_flat, j, n_pages)
            c_s, buf_s = _convmod_sample(glu_s.reshape(bs, ts, 2 * b_ch), cache_convmod[j],
                                         conv_w_b[j], conv_b_b[j], ln_g_b[j], ln_b_b[j])
            cat_s = jnp.concatenate([oa_s.reshape(ms, A_WIDTH), c_s.reshape(ms, b_ch)], axis=1)
            xp, xs = _proj([oa_p, c_p], cat_s, w_out_a, j, res=(xp, xs), name="out_a")
            outs["mk_p"].append(kp.reshape(bp, tp, A_HEADS, A_HEAD_DIM))
            outs["mv_p"].append(vp.reshape(bp, tp, A_HEADS, A_HEAD_DIM))
            outs["mk_s"].append(ks.reshape(bs, ts, A_HEADS, A_HEAD_DIM))
            outs["mv_s"].append(vs.reshape(bs, ts, A_HEADS, A_HEAD_DIM))
            outs["cv_p"].append(buf_p)
            outs["cv_s"].append(buf_s)
        else:
            zp, zs = _proj([hp], hs, w_in_c, j, name="in_c")
            og_p, st_p = _hgrn_prompt(zp, bp, tp, c_heads, lower_bounds, gnorm_c[j], l)
            og_s, st_s = _hgrn_sample(zs.reshape(bs, ts, -1), c_heads, lower_bounds, gnorm_c[j],
                                      state_hgrn, j)
            xp, xs = _proj([og_p], og_s.reshape(ms, -1), w_out_c, j, res=(xp, xs), name="out_c")
            outs["hg_p"].append(st_p)
            outs["hg_s"].append(st_s)

        hm = _rmsnorm(mem2, g_mem_kv[l], BF16)
        mk = _mm(hm, w_mem_kv, l, tm=bp * n_mem, tn=MM_TN, n_off=0, n=mw, name="mem_k")
        mv = _mm(hm, w_mem_kv, l, tm=bp * n_mem, tn=MM_TN, n_off=mw, n=mw, name="mem_v")
        hp = _rmsnorm(xp, g_mem_q[l], BF16)
        hs = _rmsnorm(xs, g_mem_q[l], F32)
        qp, qs = _proj([hp], hs, w_mem_q, l, out_dtype=BF16, name="mem_q")
        ap = _mem_attn_prompt(qp, mk, mv, bp, tp, n_mem)
        a_s = _mem_attn_sample(qs.reshape(bs, ts, mw), cmk, cmv, l)
        xp, xs = _proj([ap], a_s.reshape(ms, mw), w_mem_o, l, res=(xp, xs), name="mem_o")
        outs["memk"].append(mk.reshape(bp, n_mem, MEM_HEADS, mw // MEM_HEADS))
        outs["memv"].append(mv.reshape(bp, n_mem, MEM_HEADS, mw // MEM_HEADS))

        hp = _rmsnorm(xp, g_ffn[l], BF16)
        hs = _rmsnorm(xs, g_ffn[l], F32)
        hid_p, fb_p, gs, us = _ffn_in(hp, hs, w_ffn_in, l, ffn_conv_w[l], ffn_conv_b[l], tp, dff)
        hid_s, fb_s = _ffn_act_sample(gs.reshape(bs, ts, dff), us.reshape(bs, ts, dff),
                                      cache_ffn_conv, l, ffn_conv_w[l], ffn_conv_b[l])
        xp = _mm(hid_p, w_ffn_out_bf, l, tm=MM_TM, tn=MM_TN, tk=dff // 2, res=xp,
                 name="ffn_out_prompt")
        xs = _mm(hid_s.reshape(ms, dff), w_ffn_out_bf, l, tm=ms, tn=MM_TN, tk=dff // 2, res=xs,
                 name="ffn_out_sample")
        outs["ff_p"].append(fb_p)
        outs["ff_s"].append(fb_s)

    y_prompt = _rmsnorm(xp, g_final, F32).reshape(bp, tp, d)
    y_sample = _rmsnorm(xs, g_final, F32).reshape(bs, ts, d)
    st = jnp.stack
    return (y_prompt, y_sample, st(outs["mk_p"]), st(outs["mv_p"]), st(outs["mk_s"]), st(outs["mv_s"]),
            st(outs["cv_p"]), st(outs["cv_s"]), st(outs["hg_p"]), st(outs["hg_s"]),
            st(outs["memk"]), st(outs["memv"]), st(outs["ff_p"]), st(outs["ff_s"]))
```

```python
import functools
import math

import jax
import jax.numpy as jnp
from jax import lax
from jax.experimental import pallas as pl
from jax.experimental.pallas import tpu as pltpu

F32 = jnp.float32
BF16 = jnp.bfloat16

EPS = 1e-6
NEG = -1e30

PAGE_SIZE = 128
A_HEADS = 16
A_HEAD_DIM = 128
A_WIDTH = A_HEADS * A_HEAD_DIM
MOBA_BLOCK = 256
MOBA_TOPK = 3
B_CONV_WIDTH = 31
C_KEY_DIM = 128
MEM_HEADS = 4
FFN_CONV_WIDTH = 3

LANES = 128
F32_SUBLANES = 8
BF16_SUBLANES = 16
VMEM_LIMIT = 60 * 1024 * 1024

NT_DIMS = (((1,), (1,)), ((), ()))
TN_DIMS = (((0,), (0,)), ((), ()))


def _params(sem, vmem=VMEM_LIMIT):
    return pltpu.CompilerParams(dimension_semantics=sem, vmem_limit_bytes=vmem)


def _sigmoid(x):
    return 1.0 / (1.0 + jnp.exp(-x))


def _split3(a):
    a1 = a.astype(BF16)
    r1 = a - a1.astype(F32)
    a2 = r1.astype(BF16)
    a3 = (r1 - a2.astype(F32)).astype(BF16)
    return a1, a2, a3


def _dot_f32(a, b, dims):
    a1, a2, a3 = _split3(a)
    b1, b2, b3 = _split3(b)
    d = lambda x, y: lax.dot_general(x, y, dims, preferred_element_type=F32)
    low = d(a1, b3) + d(a3, b1) + d(a2, b2)
    mid = d(a1, b2) + d(a2, b1)
    return (low + mid) + d(a1, b1)


def _rmsnorm_kernel(x_ref, g_ref, o_ref):
    x = x_ref[...]
    ms = jnp.mean(x * x, axis=-1, keepdims=True)
    o_ref[...] = (x * lax.rsqrt(ms + EPS) * g_ref[...]).astype(o_ref.dtype)


def _rmsnorm(x, g, out_dtype):
    m, d = x.shape
    tm = min(m, 512)
    return pl.pallas_call(
        _rmsnorm_kernel,
        grid=(m // tm,),
        in_specs=[pl.BlockSpec((tm, d), lambda i: (i, 0)),
                  pl.BlockSpec((1, d), lambda i: (0, 0))],
        out_specs=pl.BlockSpec((tm, d), lambda i: (i, 0)),
        out_shape=jax.ShapeDtypeStruct((m, d), out_dtype),
        compiler_params=_params(("parallel",)),
        name="rmsnorm",
    )(x, g.reshape(1, d))


PROJ_TM = 2048
PROJ_TN = 256
PROJ_TK = 1024
FFN_SLAB = 512
MM_TM = 1024
MM_TN = 512


def _dot_cast_chunks(x_ref, w_ref, k0, kw, rows=slice(None), keep=None):
    acc = None
    for c0 in range(0, kw, PROJ_TK):
        c1 = min(c0 + PROJ_TK, kw)
        wb = w_ref[0, k0 + c0:k0 + c1, :].astype(BF16)
        if keep is not None:
            keep[c0:c1, :] = wb
        part = jnp.dot(x_ref[rows, c0:c1], wb, preferred_element_type=F32)
        acc = part if acc is None else acc + part
    return acc


def _sample_rows(store, compute):
    @pl.when(pl.program_id(0) == 0)
    def _():
        store(compute())

    @pl.when(pl.program_id(0) != 0)
    def _():
        store(None)


def _proj_kernel(*refs, k_parts, has_res):
    nparts = len(k_parts)
    xp_refs = refs[:nparts]
    if has_res:
        xs_ref, w_ref, rp_ref, rs_ref, yp_ref, ys_ref = refs[nparts:]
    else:
        xs_ref, w_ref, yp_ref, ys_ref = refs[nparts:]

    acc = None
    k0 = 0
    for xp_ref, kw in zip(xp_refs, k_parts):
        part = _dot_cast_chunks(xp_ref, w_ref, k0, kw)
        acc = part if acc is None else acc + part
        k0 += kw
    if has_res:
        acc = acc + rp_ref[...]
    yp_ref[...] = acc.astype(yp_ref.dtype)

    def sample():
        ys = jnp.dot(xs_ref[...].astype(BF16), w_ref[0].astype(BF16), preferred_element_type=F32)
        return ys + rs_ref[...] if has_res else ys

    def store(ys):
        ys_ref[0] = jnp.zeros(ys_ref.shape[1:], F32) if ys is None else ys

    _sample_rows(store, sample)


def _proj(xp_parts, xs, w, layer, *, n_off=0, n=None, res=None, out_dtype=F32, name="proj"):
    mp = xp_parts[0].shape[0]
    k_parts = tuple(x.shape[1] for x in xp_parts)
    kdim = sum(k_parts)
    ms = xs.shape[0]
    n = w.shape[2] if n is None else n
    tm, tn = PROJ_TM, PROJ_TN
    assert mp % tm == 0 and n % tn == 0 and n_off % tn == 0 and w.shape[1] == kdim
    assert xs.shape[1] == kdim
    joff = n_off // tn
    in_specs = [pl.BlockSpec((tm, kw), lambda i, j: (i, 0)) for kw in k_parts]
    in_specs += [pl.BlockSpec((ms, kdim), lambda i, j: (0, 0)),
                 pl.BlockSpec((1, kdim, tn), lambda i, j: (layer, 0, j + joff))]
    args = list(xp_parts) + [xs, w]
    if res is not None:
        in_specs += [pl.BlockSpec((tm, tn), lambda i, j: (i, j)),
                     pl.BlockSpec((ms, tn), lambda i, j: (0, j))]
        args += list(res)
    yp, ys = pl.pallas_call(
        functools.partial(_proj_kernel, k_parts=k_parts, has_res=res is not None),
        grid=(mp // tm, n // tn),
        in_specs=in_specs,
        out_specs=[pl.BlockSpec((tm, tn), lambda i, j: (i, j)),
                   pl.BlockSpec((1, ms, tn), lambda i, j: (i, 0, j))],
        out_shape=[jax.ShapeDtypeStruct((mp, n), out_dtype),
                   jax.ShapeDtypeStruct((mp // tm, ms, n), F32)],
        compiler_params=_params(("parallel", "parallel")),
        name=name,
    )(*args)
    return yp, ys[0]


def _mm_kernel(x_ref, w_ref, *rest, nk, has_res):
    if has_res:
        r_ref, o_ref = rest
    else:
        (o_ref,) = rest
    acc = jnp.dot(x_ref[...].astype(BF16), w_ref[0].astype(BF16), preferred_element_type=F32)
    if nk == 1:
        if has_res:
            acc = acc + r_ref[...]
        o_ref[...] = acc.astype(o_ref.dtype)
    else:
        k = pl.program_id(2)

        @pl.when(k == 0)
        def _():
            o_ref[...] = (acc + r_ref[...]) if has_res else acc

        @pl.when(k > 0)
        def _():
            o_ref[...] += acc


def _mm(x, w, layer, *, tm, tn, tk=None, n_off=0, n=None, res=None, out_dtype=F32, name="matmul"):
    m, kdim = x.shape
    n = w.shape[2] if n is None else n
    tk = kdim if tk is None else tk
    nk = kdim // tk
    assert m % tm == 0 and n % tn == 0 and kdim % tk == 0 and n_off % tn == 0
    assert nk == 1 or out_dtype == F32
    joff = n_off // tn
    in_specs = [pl.BlockSpec((tm, tk), lambda i, j, k: (i, k)),
                pl.BlockSpec((1, tk, tn), lambda i, j, k: (layer, k, j + joff))]
    args = [x, w]
    if res is not None:
        in_specs.append(pl.BlockSpec((tm, tn), lambda i, j, k: (i, j)))
        args.append(res)
    return pl.pallas_call(
        functools.partial(_mm_kernel, nk=nk, has_res=res is not None),
        grid=(m // tm, n // tn, nk),
        in_specs=in_specs,
        out_specs=pl.BlockSpec((tm, tn), lambda i, j, k: (i, j)),
        out_shape=jax.ShapeDtypeStruct((m, n), out_dtype),
        compiler_params=_params(("parallel", "parallel", "arbitrary")),
        name=name,
    )(*args)


def _moba_prompt_kernel(q_ref, k_ref, v_ref, o_ref, *, nblk):
    blk = MOBA_BLOCK
    scale = 1.0 / math.sqrt(A_HEAD_DIM)
    k = k_ref[...]
    kb = k.astype(BF16)
    vb = v_ref[...].astype(BF16)
    kmean = jnp.mean(k.reshape(nblk, blk, A_HEAD_DIM), axis=1)
    krow = lax.broadcasted_iota(jnp.int32, (blk, blk), 0)
    qcol = lax.broadcasted_iota(jnp.int32, (blk, blk), 1)
    causal = krow <= qcol
    for qb in range(nblk):
        qf = q_ref[qb * blk:(qb + 1) * blk, :]
        qbf = (qf * scale).astype(BF16)
        sel = None
        if qb > MOBA_TOPK:
            gate = _dot_f32(kmean, qf, NT_DIMS)
            rows = [gate[n:n + 1, :] for n in range(qb)]
            sel = []
            for n in range(qb):
                rank = jnp.zeros((1, blk), jnp.int32)
                for n2 in range(qb):
                    if n2 == n:
                        continue
                    beats = (rows[n2] >= rows[n]) if n2 < n else (rows[n2] > rows[n])
                    rank = rank + beats.astype(jnp.int32)
                sel.append(rank < MOBA_TOPK)
        s_list = []
        for n in range(qb + 1):
            s = lax.dot_general(kb[n * blk:(n + 1) * blk], qbf, NT_DIMS,
                                preferred_element_type=F32)
            if n == qb:
                s = jnp.where(causal, s, NEG)
            elif sel is not None:
                s = jnp.where(sel[n], s, NEG)
            s_list.append(s)
        m = s_list[0].max(axis=0, keepdims=True)
        for s in s_list[1:]:
            m = jnp.maximum(m, s.max(axis=0, keepdims=True))
        l = jnp.zeros((1, blk), F32)
        ot = jnp.zeros((A_HEAD_DIM, blk), F32)
        for n, s in enumerate(s_list):
            p = jnp.exp(s - m)
            l = l + p.sum(axis=0, keepdims=True)
            ot = ot + lax.dot_general(vb[n * blk:(n + 1) * blk], p.astype(BF16), TN_DIMS,
                                      preferred_element_type=F32)
        o_ref[qb * blk:(qb + 1) * blk, :] = (ot / l).T.astype(o_ref.dtype)


def _moba_prompt(q, k, v, nb, t):
    hd = A_HEAD_DIM
    spec = lambda: pl.BlockSpec((t, hd), lambda b, h: (b, h))
    return pl.pallas_call(
        functools.partial(_moba_prompt_kernel, nblk=t // MOBA_BLOCK),
        grid=(nb, A_HEADS),
        in_specs=[spec(), spec(), spec()],
        out_specs=spec(),
        out_shape=jax.ShapeDtypeStruct((nb * t, A_WIDTH), BF16),
        compiler_params=_params(("parallel", "parallel")),
        name="moba_prompt",
    )(q, k, v)


PAGES_PER_BLOCK = MOBA_BLOCK // PAGE_SIZE
KMEAN_BLOCKS = 2


def _kmean_kernel(pt_ref, *refs):
    o_ref = refs[-1]
    for i in range(KMEAN_BLOCKS):
        s = jnp.zeros((A_HEADS, A_HEAD_DIM), F32)
        for pg in range(PAGES_PER_BLOCK):
            s = s + jnp.sum(refs[i * PAGES_PER_BLOCK + pg][0, 0], axis=0)
        o_ref[0, i] = s * (1.0 / MOBA_BLOCK)


def _moba_kmean(cache_k, pt_flat, j, nb, n_pages):
    nblk = n_pages // PAGES_PER_BLOCK
    per_step = KMEAN_BLOCKS * PAGES_PER_BLOCK
    page = lambda pg: pl.BlockSpec(
        (1, 1, PAGE_SIZE, A_HEADS, A_HEAD_DIM),
        lambda b, n, pt: (j, pt[b * n_pages + per_step * n + pg], 0, 0, 0))
    return pl.pallas_call(
        _kmean_kernel,
        grid_spec=pltpu.PrefetchScalarGridSpec(
            num_scalar_prefetch=1,
            grid=(nb, nblk // KMEAN_BLOCKS),
            in_specs=[page(pg) for pg in range(per_step)],
            out_specs=pl.BlockSpec((1, KMEAN_BLOCKS, A_HEADS, A_HEAD_DIM),
                                   lambda b, n, pt: (b, n, 0, 0)),
        ),
        out_shape=jax.ShapeDtypeStruct((nb, nblk, A_HEADS, A_HEAD_DIM), F32),
        compiler_params=_params(("parallel", "parallel")),
        name="moba_kmean",
    )(pt_flat, *([cache_k] * per_step))


def _moba_select_kernel(q_ref, km_ref, o_ref, *, ts, nblk):
    q = q_ref[0]
    km = km_ref[0]
    nq = ts * A_HEADS
    qrep = jnp.concatenate(
        [jnp.broadcast_to(q[t:t + 1], (A_HEADS, A_WIDTH)) for t in range(ts)], axis=0)
    rowh = lax.broadcasted_iota(jnp.int32, (nq, A_WIDTH), 0) % A_HEADS
    colh = lax.broadcasted_iota(jnp.int32, (nq, A_WIDTH), 1) // A_HEAD_DIM
    qbig = jnp.where(rowh == colh, qrep, 0.0)
    gate = _dot_f32(km, qbig, NT_DIMS)
    nidx = lax.broadcasted_iota(jnp.int32, (nblk, nq), 0)
    for r in range(MOBA_TOPK):
        m = gate.max(axis=0, keepdims=True)
        idx = jnp.min(jnp.where(gate == m, nidx, nblk), axis=0, keepdims=True)
        o_ref[0, r:r + 1, :] = idx
        gate = jnp.where(nidx == idx, -jnp.inf, gate)


def _moba_select(qs3, kmean):
    nb, ts, _ = qs3.shape
    nblk = kmean.shape[1]
    nq = ts * A_HEADS
    return pl.pallas_call(
        functools.partial(_moba_select_kernel, ts=ts, nblk=nblk),
        grid=(nb,),
        in_specs=[pl.BlockSpec((1, ts, A_WIDTH), lambda b: (b, 0, 0)),
                  pl.BlockSpec((1, nblk, A_WIDTH), lambda b: (b, 0, 0))],
        out_specs=pl.BlockSpec((1, MOBA_TOPK, nq), lambda b: (b, 0, 0)),
        out_shape=jax.ShapeDtypeStruct((nb, MOBA_TOPK, nq), jnp.int32),
        compiler_params=_params(("parallel",)),
        name="moba_select",
    )(qs3, kmean)


def _moba_sample_kernel(sel_ref, pt_ref, q_ref, kn_ref, vn_ref, ck_hbm, cv_hbm, o_ref,
                        kbuf, vbuf, sems, *, j, ts, n_pages):
    b = pl.program_id(0)
    h = pl.program_id(1)
    nblk = n_pages // PAGES_PER_BLOCK
    nq = ts * A_HEADS
    step = b * A_HEADS + h
    nsteps = pl.num_programs(0) * A_HEADS
    half = step % 2

    def page_copies(bi, hi, hf):
        copies = []
        for t in range(ts):
            for r in range(MOBA_TOPK):
                blkid = jnp.minimum(sel_ref[bi * (MOBA_TOPK * nq) + r * nq + t * A_HEADS + hi], nblk - 1)
                for pg in range(PAGES_PER_BLOCK):
                    page = pt_ref[bi * n_pages + blkid * PAGES_PER_BLOCK + pg]
                    slot = (t * MOBA_TOPK + r) * PAGES_PER_BLOCK + pg
                    copies.append(pltpu.make_async_copy(
                        ck_hbm.at[j, page, :, hi, :], kbuf.at[hf, slot], sems.at[hf]))
                    copies.append(pltpu.make_async_copy(
                        cv_hbm.at[j, page, :, hi, :], vbuf.at[hf, slot], sems.at[hf]))
        return copies

    @pl.when(step == 0)
    def _():
        for c in page_copies(b, h, half):
            c.start()

    @pl.when(step + 1 < nsteps)
    def _():
        nxt = step + 1
        for c in page_copies(nxt // A_HEADS, nxt % A_HEADS, 1 - half):
            c.start()

    for c in page_copies(b, h, half):
        c.wait()

    scale = 1.0 / math.sqrt(A_HEAD_DIM)
    npg = MOBA_TOPK * PAGES_PER_BLOCK
    trow = lax.broadcasted_iota(jnp.int32, (ts, 1), 0)
    for t in range(ts):
        q = q_ref[0, t:t + 1, :] * scale
        slots = range(t * npg, (t + 1) * npg)
        s_cols = [jnp.sum(kbuf[half, sl] * q, axis=1, keepdims=True) for sl in slots]
        s_own = jnp.sum(kn_ref[0] * q, axis=1, keepdims=True)
        s_own = jnp.where(trow <= t, s_own, NEG)
        m = s_own.max(axis=0, keepdims=True)
        for s in s_cols:
            m = jnp.maximum(m, s.max(axis=0, keepdims=True))
        p_own = jnp.exp(s_own - m)
        l = p_own.sum(axis=0, keepdims=True)
        o = jnp.sum(p_own * vn_ref[0], axis=0, keepdims=True)
        for s, sl in zip(s_cols, slots):
            p = jnp.exp(s - m)
            l = l + p.sum(axis=0, keepdims=True)
            o = o + jnp.sum(p * vbuf[half, sl], axis=0, keepdims=True)
        o_ref[0, t:t + 1, :] = o / l


def _moba_sample(qs3, ks3, vs3, cache_k, cache_v, sel_flat, pt_flat, j, n_pages):
    nb, ts, _ = qs3.shape
    hd = A_HEAD_DIM
    nslot = ts * MOBA_TOPK * PAGES_PER_BLOCK
    own = lambda: pl.BlockSpec((1, ts, hd), lambda b, h, sel, pt: (b, 0, h))
    hbm = lambda: pl.BlockSpec(memory_space=pl.ANY)
    return pl.pallas_call(
        functools.partial(_moba_sample_kernel, j=j, ts=ts, n_pages=n_pages),
        grid_spec=pltpu.PrefetchScalarGridSpec(
            num_scalar_prefetch=2,
            grid=(nb, A_HEADS),
            in_specs=[own(), own(), own(), hbm(), hbm()],
            out_specs=own(),
            scratch_shapes=[pltpu.VMEM((2, nslot, PAGE_SIZE, hd), F32),
                            pltpu.VMEM((2, nslot, PAGE_SIZE, hd), F32),
                            pltpu.SemaphoreType.DMA((2,))],
        ),
        out_shape=jax.ShapeDtypeStruct((nb, ts, A_WIDTH), F32),
        compiler_params=_params(("arbitrary", "arbitrary")),
        name="moba_sample",
    )(sel_flat, pt_flat, qs3, ks3, vs3, cache_k, cache_v)


def _layernorm_swish(y, g, b):
    mu = jnp.mean(y, axis=-1, keepdims=True)
    yc = y - mu
    var = jnp.mean(yc * yc, axis=-1, keepdims=True)
    yn = yc * lax.rsqrt(var + EPS) * g + b
    return yn * _sigmoid(yn)


CONV_HALO = 32
CONV_ROWS = 128
CONV_COLS = 128


def _convmod_prompt_kernel(ga_ref, gb_ref, cw_ref, cb_ref, lg_ref, lb_ref, c_ref, nb_ref,
                           ubuf, ybuf, wbuf, *, tt, nt):
    w = B_CONV_WIDTH
    ch = ubuf.shape[1]
    t = pl.program_id(1)

    @pl.when(t == 0)
    def _():
        ubuf[0:CONV_HALO, :] = jnp.zeros((CONV_HALO, ch), F32)

    ubuf[CONV_HALO:CONV_HALO + tt, :] = ga_ref[...] * _sigmoid(gb_ref[...])
    base = CONV_HALO - (w - 1)
    for c0 in range(0, ch, CONV_COLS):
        cols = slice(c0, c0 + CONV_COLS)
        for r0 in range(0, tt, CONV_ROWS):
            acc = jnp.broadcast_to(cb_ref[:, cols], (CONV_ROWS, CONV_COLS))
            for s in range(F32_SUBLANES):
                taps = range(s, w, F32_SUBLANES)
                start = r0 + base + s
                rows = CONV_ROWS + F32_SUBLANES * (len(taps) - 1)
                wbuf[s, 0:rows, :] = ubuf[start:start + rows, cols]
                for a, j in enumerate(taps):
                    acc = acc + cw_ref[j:j + 1, cols] * \
                        wbuf[s, a * F32_SUBLANES:a * F32_SUBLANES + CONV_ROWS, :]
            ybuf[r0:r0 + CONV_ROWS, cols] = acc
    c_ref[...] = _layernorm_swish(ybuf[...], lg_ref[...], lb_ref[...]).astype(c_ref.dtype)

    @pl.when(t == nt - 1)
    def _():
        nb_ref[0] = ubuf[CONV_HALO + tt - (w - 1):CONV_HALO + tt, :]

    ubuf[0:CONV_HALO, :] = ubuf[tt:tt + CONV_HALO, :]


def _convmod_prompt(glu, nb, t, cw, cb, lg, lb):
    ch = glu.shape[1] // 2
    tt = 256
    nt = t // tt
    vec = lambda: pl.BlockSpec((1, ch), lambda b, i: (0, 0))
    return pl.pallas_call(
        functools.partial(_convmod_prompt_kernel, tt=tt, nt=nt),
        grid=(nb, nt),
        in_specs=[pl.BlockSpec((tt, ch), lambda b, i: (b * nt + i, 0)),
                  pl.BlockSpec((tt, ch), lambda b, i: (b * nt + i, 1)),
                  pl.BlockSpec((B_CONV_WIDTH, ch), lambda b, i: (0, 0)),
                  vec(), vec(), vec()],
        out_specs=[pl.BlockSpec((tt, ch), lambda b, i: (b * nt + i, 0)),
                   pl.BlockSpec((1, B_CONV_WIDTH - 1, ch), lambda b, i: (b, 0, 0))],
        out_shape=[jax.ShapeDtypeStruct((nb * t, ch), BF16),
                   jax.ShapeDtypeStruct((nb, B_CONV_WIDTH - 1, ch), F32)],
        scratch_shapes=[pltpu.VMEM((CONV_HALO + tt, ch), F32), pltpu.VMEM((tt, ch), F32),
                        pltpu.VMEM((F32_SUBLANES, CONV_ROWS + CONV_HALO, CONV_COLS), F32)],
        compiler_params=_params(("parallel", "arbitrary")),
        name="convmod_prompt",
    )(glu, glu, cw, cb.reshape(1, ch), lg.reshape(1, ch), lb.reshape(1, ch))


def _convmod_sample_kernel(ga_ref, gb_ref, cache_ref, cw_ref, cb_ref, lg_ref, lb_ref,
                           c_ref, nb_ref, ubuf, *, ts):
    w = B_CONV_WIDTH
    ch = ubuf.shape[1]
    ubuf[0:w - 1, :] = cache_ref[0]
    ubuf[w - 1:w - 1 + ts, :] = ga_ref[0] * _sigmoid(gb_ref[0])
    acc = jnp.broadcast_to(cb_ref[...], (ts, ch))
    for j in range(w):
        acc = acc + cw_ref[j:j + 1, :] * ubuf[j:j + ts, :]
    c_ref[0] = _layernorm_swish(acc, lg_ref[...], lb_ref[...])
    nb_ref[0] = ubuf[ts:ts + w - 1, :]


def _convmod_sample(glu3, cache, cw, cb, lg, lb):
    nb, ts, ch2 = glu3.shape
    ch = ch2 // 2
    w = B_CONV_WIDTH
    vec = lambda: pl.BlockSpec((1, ch), lambda b: (0, 0))
    return pl.pallas_call(
        functools.partial(_convmod_sample_kernel, ts=ts),
        grid=(nb,),
        in_specs=[pl.BlockSpec((1, ts, ch), lambda b: (b, 0, 0)),
                  pl.BlockSpec((1, ts, ch), lambda b: (b, 0, 1)),
                  pl.BlockSpec((1, w - 1, ch), lambda b: (b, 0, 0)),
                  pl.BlockSpec((w, ch), lambda b: (0, 0)),
                  vec(), vec(), vec()],
        out_specs=[pl.BlockSpec((1, ts, ch), lambda b: (b, 0, 0)),
                   pl.BlockSpec((1, w - 1, ch), lambda b: (b, 0, 0))],
        out_shape=[jax.ShapeDtypeStruct((nb, ts, ch), F32),
                   jax.ShapeDtypeStruct((nb, w - 1, ch), F32)],
        scratch_shapes=[pltpu.VMEM((w - 1 + ts + 5, ch), F32)],
        compiler_params=_params(("parallel",)),
        name="convmod_sample",
    )(glu3, glu3, cache, cw, cb.reshape(1, ch), lg.reshape(1, ch), lb.reshape(1, ch))


HG_CHUNK = 128
HG_SUB = 16
HG_HEADS = 8


def _hgrn_lower_bound(lbp, layer):
    e = jnp.exp(lbp - lbp.max(axis=0, keepdims=True))
    sm = e / e.sum(axis=0, keepdims=True)
    lb = jnp.zeros_like(sm[0:1])
    for i in range(1, layer + 1):
        lb = lb + sm[i:i + 1]
    return lb


def _hgrn_out(o, g, gn):
    ms = jnp.mean(o * o, axis=-1, keepdims=True)
    return (o * lax.rsqrt(ms + EPS) * gn) * (g * _sigmoid(g))


def _hgrn_chunk(q, f, v, g, lb, gn, s_prev, kpad, bpad, consts):
    cs, sub, kd = HG_CHUNK, HG_SUB, C_KEY_DIM
    tril, rowk, eye, lane, blockdiag = consts
    fg = lb + (1.0 - lb) * _sigmoid(f)
    kk = 1.0 - fg
    vb = v.astype(BF16)
    l1, l2, l3 = _split3(jnp.log(fg))
    cum = lambda x: jnp.dot(tril, x, preferred_element_type=F32)
    b = (cum(l3) + cum(l2)) + cum(l1)

    o = jnp.dot((q * jnp.exp(b)).astype(BF16), s_prev.astype(BF16), preferred_element_type=F32)

    bref_rows = jnp.concatenate(
        [jnp.zeros((sub, kd), F32)]
        + [jnp.broadcast_to(b[sub * i - 1:sub * i], (sub, kd)) for i in range(1, cs // sub)], axis=0)
    qe = (q * jnp.exp(b - bref_rows)).astype(BF16)
    parts = [jnp.zeros((sub, cs), F32)]
    for i in range(1, cs // sub):
        bref = b[sub * i - 1:sub * i]
        ke = jnp.where(rowk < sub * i, kk * jnp.exp(jnp.minimum(bref - b, 0.0)), 0.0).astype(BF16)
        parts.append(lax.dot_general(qe[sub * i:sub * (i + 1)], ke, NT_DIMS,
                                     preferred_element_type=F32))
    a = jnp.concatenate(parts, axis=0)

    kpad[sub:, :] = kk
    bpad[sub:, :] = b
    band = jnp.zeros((cs, cs), F32)
    for d in range(sub):
        if d == 0:
            p = q * kk
        else:
            p = q * kpad[sub - d:sub - d + cs, :] * jnp.exp(b - bpad[sub - d:sub - d + cs, :])
        band = jnp.where(lane == (cs - d) % cs, p.sum(axis=1, keepdims=True), band)
    band = pltpu.roll(band, 0, 1, stride=1, stride_axis=0)
    a = a + jnp.where(blockdiag, band, 0.0)
    o = o + jnp.dot(a.astype(BF16), vb, preferred_element_type=F32)

    bl = b[cs - 1:cs]
    kd_mat = (kk * jnp.exp(bl - b)).astype(BF16)
    decay_col = jnp.sum(jnp.where(eye, jnp.broadcast_to(jnp.exp(bl), (kd, kd)), 0.0),
                        axis=1, keepdims=True)
    s_new = decay_col * s_prev + lax.dot_general(kd_mat, vb, TN_DIMS, preferred_element_type=F32)
    return _hgrn_out(o, g, gn), s_new


def _hgrn_prompt_kernel(q_ref, f_ref, i_ref, g_ref, lbp_ref, gn_ref, o_ref, s_ref,
                        kpad, bpad, *, layer):
    cs, sub, kd = HG_CHUNK, HG_SUB, C_KEY_DIM

    @pl.when(pl.program_id(2) == 0)
    def _():
        s_ref[...] = jnp.zeros_like(s_ref)

    rr = lax.broadcasted_iota(jnp.int32, (cs, cs), 0)
    cc = lax.broadcasted_iota(jnp.int32, (cs, cs), 1)
    consts = (jnp.where(cc <= rr, 1.0, 0.0).astype(BF16),
              lax.broadcasted_iota(jnp.int32, (cs, kd), 0),
              rr == cc,
              cc,
              (rr // sub == cc // sub) & (cc <= rr))
    kpad[:, 0:sub, :] = jnp.zeros((HG_HEADS, sub, kd), F32)
    bpad[:, 0:sub, :] = jnp.zeros((HG_HEADS, sub, kd), F32)
    lbs = _hgrn_lower_bound(lbp_ref[...], layer)
    for h in range(HG_HEADS):
        cols = slice(h * kd, (h + 1) * kd)
        og, s_new = _hgrn_chunk(q_ref[:, cols], f_ref[:, cols], i_ref[:, cols], g_ref[:, cols],
                                lbs[:, cols], gn_ref[...], s_ref[0, h],
                                kpad.at[h], bpad.at[h], consts)
        s_ref[0, h] = s_new
        o_ref[:, cols] = og.astype(o_ref.dtype)


def _hgrn_prompt(z, nb, t, heads, lower_bounds, gnorm, layer):
    kd = C_KEY_DIM
    assert HG_CHUNK == kd and heads % HG_HEADS == 0
    nchunk = t // HG_CHUNK
    depth = lower_bounds.shape[0]
    hw = HG_HEADS * kd
    ng = heads // HG_HEADS
    col = lambda part: pl.BlockSpec((HG_CHUNK, hw), lambda b, h, c: (b * nchunk + c, part * ng + h))
    return pl.pallas_call(
        functools.partial(_hgrn_prompt_kernel, layer=layer),
        grid=(nb, ng, nchunk),
        in_specs=[col(0), col(1), col(2), col(3),
                  pl.BlockSpec((depth, hw), lambda b, h, c: (0, h)),
                  pl.BlockSpec((1, kd), lambda b, h, c: (0, 0))],
        out_specs=[pl.BlockSpec((HG_CHUNK, hw), lambda b, h, c: (b * nchunk + c, h)),
                   pl.BlockSpec((1, HG_HEADS, kd, kd), lambda b, h, c: (b, h, 0, 0))],
        out_shape=[jax.ShapeDtypeStruct((nb * t, heads * kd), BF16),
                   jax.ShapeDtypeStruct((nb, heads, kd, kd), F32)],
        scratch_shapes=[pltpu.VMEM((HG_HEADS, HG_SUB + HG_CHUNK, kd), F32),
                        pltpu.VMEM((HG_HEADS, HG_SUB + HG_CHUNK, kd), F32)],
        compiler_params=_params(("parallel", "parallel", "arbitrary")),
        name="hgrn_prompt",
    )(z, z, z, z, lower_bounds, gnorm.reshape(1, kd))


def _hgrn_sample_kernel(q_ref, f_ref, i_ref, g_ref, lbp_ref, gn_ref, s0_ref, o_ref, s_ref,
                        *, layer, ts):
    kd = C_KEY_DIM
    lbs = _hgrn_lower_bound(lbp_ref[...], layer)
    eye = lax.broadcasted_iota(jnp.int32, (kd, kd), 0) == lax.broadcasted_iota(jnp.int32, (kd, kd), 1)
    col = lambda x: jnp.sum(jnp.where(eye, jnp.broadcast_to(x, (kd, kd)), 0.0), axis=1, keepdims=True)
    for h in range(HG_HEADS):
        cols = slice(h * kd, (h + 1) * kd)
        lb = lbs[:, cols]
        fg = lb + (1.0 - lb) * _sigmoid(f_ref[0, :, cols])
        kk = 1.0 - fg
        q = q_ref[0, :, cols]
        v = i_ref[0, :, cols]
        g = g_ref[0, :, cols]
        s = s0_ref[0, h]
        for t in range(ts):
            s = col(fg[t:t + 1]) * s + col(kk[t:t + 1]) * v[t:t + 1]
            o = jnp.sum(col(q[t:t + 1]) * s, axis=0, keepdims=True)
            o_ref[0, t:t + 1, cols] = _hgrn_out(o, g[t:t + 1], gn_ref[...])
        s_ref[0, h] = s


def _hgrn_sample(zs3, heads, lower_bounds, gnorm, state, j):
    nb, ts, _ = zs3.shape
    kd = C_KEY_DIM
    depth = lower_bounds.shape[0]
    assert heads % HG_HEADS == 0
    hw = HG_HEADS * kd
    ng = heads // HG_HEADS
    st = state.reshape(-1, heads, kd, kd)
    col = lambda part: pl.BlockSpec((1, ts, hw), lambda b, h: (b, 0, part * ng + h))
    return pl.pallas_call(
        functools.partial(_hgrn_sample_kernel, layer=2 * j + 1, ts=ts),
        grid=(nb, ng),
        in_specs=[col(0), col(1), col(2), col(3),
                  pl.BlockSpec((depth, hw), lambda b, h: (0, h)),
                  pl.BlockSpec((1, kd), lambda b, h: (0, 0)),
                  pl.BlockSpec((1, HG_HEADS, kd, kd), lambda b, h: (j * nb + b, h, 0, 0))],
        out_specs=[pl.BlockSpec((1, ts, hw), lambda b, h: (b, 0, h)),
                   pl.BlockSpec((1, HG_HEADS, kd, kd), lambda b, h: (b, h, 0, 0))],
        out_shape=[jax.ShapeDtypeStruct((nb, ts, heads * kd), F32),
                   jax.ShapeDtypeStruct((nb, heads, kd, kd), F32)],
        compiler_params=_params(("parallel", "parallel")),
        name="hgrn_sample",
    )(zs3, zs3, zs3, zs3, lower_bounds, gnorm.reshape(1, kd), st)


def _softmax_attend(q, k, v):
    scale = 1.0 / math.sqrt(q.shape[-1])
    s = lax.dot_general(q.astype(BF16), k.astype(BF16), NT_DIMS, preferred_element_type=F32) * scale
    m = s.max(axis=1, keepdims=True)
    p = jnp.exp(s - m)
    l = p.sum(axis=1, keepdims=True)
    return jnp.dot(p.astype(BF16), v.astype(BF16), preferred_element_type=F32) / l


def _mem_attn_prompt_kernel(q_ref, k_ref, v_ref, o_ref):
    o_ref[...] = _softmax_attend(q_ref[...], k_ref[...], v_ref[...]).astype(o_ref.dtype)


def _mem_attn_prompt(q, mk, mv, nb, t, n_mem):
    w = q.shape[1]
    hd = w // MEM_HEADS
    tq = 512
    nt = t // tq
    return pl.pallas_call(
        _mem_attn_prompt_kernel,
        grid=(nb, MEM_HEADS, nt),
        in_specs=[pl.BlockSpec((tq, hd), lambda b, h, i: (b * nt + i, h)),
                  pl.BlockSpec((n_mem, hd), lambda b, h, i: (b, h)),
                  pl.BlockSpec((n_mem, hd), lambda b, h, i: (b, h))],
        out_specs=pl.BlockSpec((tq, hd), lambda b, h, i: (b * nt + i, h)),
        out_shape=jax.ShapeDtypeStruct((nb * t, w), BF16),
        compiler_params=_params(("parallel", "parallel", "parallel")),
        name="mem_attn_prompt",
    )(q, mk, mv)


def _mem_attn_sample_kernel(q_ref, k_ref, v_ref, o_ref):
    hd = k_ref.shape[-1]
    for h in range(MEM_HEADS):
        cols = slice(h * hd, (h + 1) * hd)
        o_ref[0, :, cols] = _softmax_attend(q_ref[0, :, cols], k_ref[0, 0, :, h, :], v_ref[0, 0, :, h, :])


def _mem_attn_sample(q3, cache_k, cache_v, layer):
    nb, ts, w = q3.shape
    n_mem = cache_k.shape[2]
    hd = w // MEM_HEADS
    kv = lambda: pl.BlockSpec((1, 1, n_mem, MEM_HEADS, hd), lambda b: (layer, b, 0, 0, 0))
    return pl.pallas_call(
        _mem_attn_sample_kernel,
        grid=(nb,),
        in_specs=[pl.BlockSpec((1, ts, w), lambda b: (b, 0, 0)), kv(), kv()],
        out_specs=pl.BlockSpec((1, ts, w), lambda b: (b, 0, 0)),
        out_shape=jax.ShapeDtypeStruct((nb, ts, w), F32),
        compiler_params=_params(("parallel",)),
        name="mem_attn_sample",
    )(q3, cache_k, cache_v)


def _ffn_in_kernel(*refs, tm, tiles_per_seq):
    if tiles_per_seq > 1:
        (xp_ref, xprev_ref, xs_ref, wg_ref, wu_ref, cw_ref, cb_ref,
         h_ref, tail_ref, gs_ref, us_ref, gbuf, wgb, wub) = refs
    else:
        (xp_ref, xs_ref, wg_ref, wu_ref, cw_ref, cb_ref,
         h_ref, tail_ref, gs_ref, us_ref, gbuf, wgb, wub) = refs
    halo = BF16_SUBLANES
    d = xp_ref.shape[1]
    tn = h_ref.shape[1]
    if tiles_per_seq > 1:
        gprev = jnp.dot(xprev_ref[...], wg_ref[0].astype(BF16), preferred_element_type=F32)
        gprev = jnp.where(pl.program_id(0) % tiles_per_seq == 0, 0.0, gprev)
    else:
        gprev = jnp.zeros((halo, tn), F32)
    gbuf[0:halo, :] = gprev
    cw = cw_ref[...]
    for r0 in range(0, tm, FFN_SLAB):
        rows = slice(r0, r0 + FFN_SLAB)
        if r0 == 0:
            gate = _dot_cast_chunks(xp_ref, wg_ref, 0, d, rows=rows, keep=wgb)
            up = _dot_cast_chunks(xp_ref, wu_ref, 0, d, rows=rows, keep=wub)
        else:
            gate = jnp.dot(xp_ref[rows, :], wgb[...], preferred_element_type=F32)
            up = jnp.dot(xp_ref[rows, :], wub[...], preferred_element_type=F32)
        gbuf[halo + r0:halo + r0 + FFN_SLAB, :] = gate
        gc = (cw[0:1] * gbuf[halo - 2 + r0:halo - 2 + r0 + FFN_SLAB, :]
              + cw[1:2] * gbuf[halo - 1 + r0:halo - 1 + r0 + FFN_SLAB, :]
              + cw[2:3] * gate + cb_ref[...])
        h_ref[rows, :] = (gc * _sigmoid(gc) * up).astype(h_ref.dtype)
    tail_ref[0] = gbuf[halo + tm - (FFN_CONV_WIDTH - 1):halo + tm, :]

    def sample():
        xs = xs_ref[...].astype(BF16)
        return (jnp.dot(xs, wgb[...], preferred_element_type=F32),
                jnp.dot(xs, wub[...], preferred_element_type=F32))

    def store(gu):
        zero = jnp.zeros(gs_ref.shape[1:], F32)
        gs_ref[0], us_ref[0] = (zero, zero) if gu is None else gu

    _sample_rows(store, sample)


def _ffn_in(hp, hs, w_in, layer, cw, cb, t, dff):
    m, d = hp.shape
    ms = hs.shape[0]
    tm, tn = PROJ_TM, PROJ_TN
    halo = BF16_SUBLANES
    assert t % tm == 0 and dff % tn == 0
    tiles_per_seq = t // tm
    ni, nj = m // tm, dff // tn
    in_specs = [pl.BlockSpec((tm, d), lambda i, j: (i, 0), pipeline_mode=pl.Buffered(1))]
    args = [hp]
    if tiles_per_seq > 1:
        in_specs.append(pl.BlockSpec((halo, d), lambda i, j: (jnp.maximum(i * (tm // halo) - 1, 0), 0)))
        args.append(hp)
    in_specs += [pl.BlockSpec((ms, d), lambda i, j: (0, 0)),
                 pl.BlockSpec((1, d, tn), lambda i, j: (layer, 0, j)),
                 pl.BlockSpec((1, d, tn), lambda i, j: (layer, 0, nj + j)),
                 pl.BlockSpec((FFN_CONV_WIDTH, tn), lambda i, j: (0, j)),
                 pl.BlockSpec((1, tn), lambda i, j: (0, j))]
    args += [hs, w_in, w_in, cw, cb.reshape(1, dff)]
    sample_out = lambda: pl.BlockSpec((1, ms, tn), lambda i, j: (i, 0, j))
    hidden, tails, gs, us = pl.pallas_call(
        functools.partial(_ffn_in_kernel, tm=tm, tiles_per_seq=tiles_per_seq),
        grid=(ni, nj),
        in_specs=in_specs,
        out_specs=[pl.BlockSpec((tm, tn), lambda i, j: (i, j)),
                   pl.BlockSpec((1, FFN_CONV_WIDTH - 1, tn), lambda i, j: (i, 0, j)),
                   sample_out(), sample_out()],
        out_shape=[jax.ShapeDtypeStruct((m, dff), BF16),
                   jax.ShapeDtypeStruct((ni, FFN_CONV_WIDTH - 1, dff), F32),
                   jax.ShapeDtypeStruct((ni, ms, dff), F32),
                   jax.ShapeDtypeStruct((ni, ms, dff), F32)],
        scratch_shapes=[pltpu.VMEM((halo + tm, tn), F32),
                        pltpu.VMEM((d, tn), BF16), pltpu.VMEM((d, tn), BF16)],
        compiler_params=_params(("parallel", "parallel")),
        name="ffn_in",
    )(*args)
    return hidden, tails[tiles_per_seq - 1::tiles_per_seq], gs[0], us[0]


def _ffn_act_sample_kernel(g_ref, u_ref, cache_ref, cw_ref, cb_ref, h_ref, nb_ref, *, ts):
    w = FFN_CONV_WIDTH
    rows = [cache_ref[0, 0, r:r + 1, :] for r in range(w - 1)] + \
           [g_ref[0, t:t + 1, :] for t in range(ts)]
    for t in range(ts):
        gc = cb_ref[...]
        for j in range(w):
            gc = gc + cw_ref[j:j + 1, :] * rows[t + j]
        h_ref[0, t:t + 1, :] = gc * _sigmoid(gc) * u_ref[0, t:t + 1, :]
    for r in range(w - 1):
        nb_ref[0, r:r + 1, :] = rows[ts + r]


def _ffn_act_sample(gs3, us3, cache, layer, cw, cb):
    nb, ts, dff = gs3.shape
    w = FFN_CONV_WIDTH
    row = lambda: pl.BlockSpec((1, ts, dff), lambda b: (b, 0, 0))
    return pl.pallas_call(
        functools.partial(_ffn_act_sample_kernel, ts=ts),
        grid=(nb,),
        in_specs=[row(), row(),
                  pl.BlockSpec((1, 1, w - 1, dff), lambda b: (layer, b, 0, 0)),
                  pl.BlockSpec((w, dff), lambda b: (0, 0)),
                  pl.BlockSpec((1, dff), lambda b: (0, 0))],
        out_specs=[row(), pl.BlockSpec((1, w - 1, dff), lambda b: (b, 0, 0))],
        out_shape=[jax.ShapeDtypeStruct((nb, ts, dff), F32),
                   jax.ShapeDtypeStruct((nb, w - 1, dff), F32)],
        compiler_params=_params(("parallel",)),
        name="ffn_act_sample",
    )(gs3, us3, cache, cw, cb.reshape(1, dff))


def kernel(x_prompt, x_sample, mem_prompt, cache_moba_k, cache_moba_v, page_table, cache_convmod,
           state_hgrn, cache_mem_k, cache_mem_v, cache_ffn_conv, g_mix, g_mem_q, g_mem_kv, g_ffn,
           g_final, w_in_a, w_out_a, conv_w_b, conv_b_b, ln_g_b, ln_b_b, w_in_c, w_out_c,
           lower_bounds, gnorm_c, w_mem_q, w_mem_kv, w_mem_o, w_ffn_in, ffn_conv_w, ffn_conv_b,
           w_ffn_out):
    bp, tp, d = x_prompt.shape
    bs, ts, _ = x_sample.shape
    depth = g_mix.shape[0]
    n_mem = mem_prompt.shape[1]
    n_pages = page_table.shape[1]
    dff = w_ffn_out.shape[1]
    b_ch = conv_w_b.shape[2]
    c_heads = w_out_c.shape[1] // C_KEY_DIM
    mw = w_mem_q.shape[2]
    mp, ms = bp * tp, bs * ts

    xp = x_prompt.reshape(mp, d)
    xs = x_sample.reshape(ms, d)
    pt_flat = page_table.reshape(-1)
    mem2 = mem_prompt.reshape(bp * n_mem, d)
    w_ffn_out_bf = w_ffn_out.astype(BF16)

    outs = {k: [] for k in ("mk_p", "mv_p", "mk_s", "mv_s", "cv_p", "cv_s", "hg_p", "hg_s",
                            "memk", "memv", "ff_p", "ff_s")}
    for l in range(depth):
        j = l // 2
        hp = _rmsnorm(xp, g_mix[l], BF16)
        hs = _rmsnorm(xs, g_mix[l], F32)
        if l % 2 == 0:
            qp, qs = _proj([hp], hs, w_in_a, j, n_off=0, n=A_WIDTH, name="in_a_q")
            kp, ks = _proj([hp], hs, w_in_a, j, n_off=A_WIDTH, n=A_WIDTH, name="in_a_k")
            vp, vs = _proj([hp], hs, w_in_a, j, n_off=2 * A_WIDTH, n=A_WIDTH, name="in_a_v")
            glu_p, glu_s = _proj([hp], hs, w_in_a, j, n_off=3 * A_WIDTH, n=2 * b_ch, name="in_a_glu")
            qs3 = qs.reshape(bs, ts, A_WIDTH)
            oa_p = _moba_prompt(qp, kp, vp, bp, tp)
            c_p, buf_p = _convmod_prompt(glu_p, bp, tp, conv_w_b[j], conv_b_b[j],
                                         ln_g_b[j], ln_b_b[j])
            kmean = _moba_kmean(cache_moba_k, pt_flat, j, bs, n_pages)
            sel = _moba_select(qs3, kmean.reshape(bs, -1, A_WIDTH))
            oa_s = _moba_sample(qs3, ks.reshape(bs, ts, A_WIDTH), vs.reshape(bs, ts, A_WIDTH),
                                cache_moba_k, cache_moba_v, sel.reshape(-1), pt_flat, j, n_pages)
            c_s, buf_s = _convmod_sample(glu_s.reshape(bs, ts, 2 * b_ch), cache_convmod[j],
                                         conv_w_b[j], conv_b_b[j], ln_g_b[j], ln_b_b[j])
            cat_s = jnp.concatenate([oa_s.reshape(ms, A_WIDTH), c_s.reshape(ms, b_ch)], axis=1)
            xp, xs = _proj([oa_p, c_p], cat_s, w_out_a, j, res=(xp, xs), name="out_a")
            outs["mk_p"].append(kp.reshape(bp, tp, A_HEADS, A_HEAD_DIM))
            outs["mv_p"].append(vp.reshape(bp, tp, A_HEADS, A_HEAD_DIM))
            outs["mk_s"].append(ks.reshape(bs, ts, A_HEADS, A_HEAD_DIM))
            outs["mv_s"].append(vs.reshape(bs, ts, A_HEADS, A_HEAD_DIM))
            outs["cv_p"].append(buf_p)
            outs["cv_s"].append(buf_s)
        else:
            zp, zs = _proj([hp], hs, w_in_c, j, name="in_c")
            og_p, st_p = _hgrn_prompt(zp, bp, tp, c_heads, lower_bounds, gnorm_c[j], l)
            og_s, st_s = _hgrn_sample(zs.reshape(bs, ts, -1), c_heads, lower_bounds, gnorm_c[j],
                                      state_hgrn, j)
            xp, xs = _proj([og_p], og_s.reshape(ms, -1), w_out_c, j, res=(xp, xs), name="out_c")
            outs["hg_p"].append(st_p)
            outs["hg_s"].append(st_s)

        hm = _rmsnorm(mem2, g_mem_kv[l], BF16)
        mk = _mm(hm, w_mem_kv, l, tm=bp * n_mem, tn=MM_TN, n_off=0, n=mw, name="mem_k")
        mv = _mm(hm, w_mem_kv, l, tm=bp * n_mem, tn=MM_TN, n_off=mw, n=mw, name="mem_v")
        hp = _rmsnorm(xp, g_mem_q[l], BF16)
        hs = _rmsnorm(xs, g_mem_q[l], F32)
        qp, qs = _proj([hp], hs, w_mem_q, l, out_dtype=BF16, name="mem_q")
        ap = _mem_attn_prompt(qp, mk, mv, bp, tp, n_mem)
        a_s = _mem_attn_sample(qs.reshape(bs, ts, mw), cache_mem_k, cache_mem_v, l)
        xp, xs = _proj([ap], a_s.reshape(ms, mw), w_mem_o, l, res=(xp, xs), name="mem_o")
        outs["memk"].append(mk.reshape(bp, n_mem, MEM_HEADS, mw // MEM_HEADS))
        outs["memv"].append(mv.reshape(bp, n_mem, MEM_HEADS, mw // MEM_HEADS))

        hp = _rmsnorm(xp, g_ffn[l], BF16)
        hs = _rmsnorm(xs, g_ffn[l], F32)
        hid_p, fb_p, gs, us = _ffn_in(hp, hs, w_ffn_in, l, ffn_conv_w[l], ffn_conv_b[l], tp, dff)
        hid_s, fb_s = _ffn_act_sample(gs.reshape(bs, ts, dff), us.reshape(bs, ts, dff),
                                      cache_ffn_conv, l, ffn_conv_w[l], ffn_conv_b[l])
        xp = _mm(hid_p, w_ffn_out_bf, l, tm=MM_TM, tn=MM_TN, tk=dff // 2, res=xp,
                 name="ffn_out_prompt")
        xs = _mm(hid_s.reshape(ms, dff), w_ffn_out_bf, l, tm=ms, tn=MM_TN, tk=dff // 2, res=xs,
                 name="ffn_out_sample")
        outs["ff_p"].append(fb_p)
        outs["ff_s"].append(fb_s)

    y_prompt = _rmsnorm(xp, g_final, F32).reshape(bp, tp, d)
    y_sample = _rmsnorm(xs, g_final, F32).reshape(bs, ts, d)
    st = jnp.stack
    return (y_prompt, y_sample, st(outs["mk_p"]), st(outs["mv_p"]), st(outs["mk_s"]), st(outs["mv_s"]),
            st(outs["cv_p"]), st(outs["cv_s"]), st(outs["hg_p"]), st(outs["hg_s"]),
            st(outs["memk"]), st(outs["memv"]), st(outs["ff_p"]), st(outs["ff_s"]))
```

```python
import functools
import math

import jax
import jax.numpy as jnp
from jax import lax
from jax.experimental import pallas as pl
from jax.experimental.pallas import tpu as pltpu

F32 = jnp.float32
BF16 = jnp.bfloat16

EPS = 1e-6
NEG = -1e30

PAGE_SIZE = 128
A_HEADS = 16
A_HEAD_DIM = 128
A_WIDTH = A_HEADS * A_HEAD_DIM
MOBA_BLOCK = 256
MOBA_TOPK = 3
B_CONV_WIDTH = 31
C_KEY_DIM = 128
MEM_HEADS = 4
FFN_CONV_WIDTH = 3

LANES = 128
F32_SUBLANES = 8
BF16_SUBLANES = 16
VMEM_LIMIT = 60 * 1024 * 1024

NT_DIMS = (((1,), (1,)), ((), ()))
TN_DIMS = (((0,), (0,)), ((), ()))


def _params(sem, vmem=VMEM_LIMIT):
    return pltpu.CompilerParams(dimension_semantics=sem, vmem_limit_bytes=vmem)


def _sigmoid(x):
    return 1.0 / (1.0 + jnp.exp(-x))


def _split3(a):
    a1 = a.astype(BF16)
    r1 = a - a1.astype(F32)
    a2 = r1.astype(BF16)
    a3 = (r1 - a2.astype(F32)).astype(BF16)
    return a1, a2, a3


def _dot_f32(a, b, dims):
    a1, a2, a3 = _split3(a)
    b1, b2, b3 = _split3(b)
    d = lambda x, y: lax.dot_general(x, y, dims, preferred_element_type=F32)
    low = d(a1, b3) + d(a3, b1) + d(a2, b2)
    mid = d(a1, b2) + d(a2, b1)
    return (low + mid) + d(a1, b1)


def _rmsnorm_kernel(x_ref, g_ref, o_ref):
    x = x_ref[...]
    ms = jnp.mean(x * x, axis=-1, keepdims=True)
    o_ref[...] = (x * lax.rsqrt(ms + EPS) * g_ref[...]).astype(o_ref.dtype)


def _rmsnorm(x, g, out_dtype):
    m, d = x.shape
    tm = min(m, 512)
    return pl.pallas_call(
        _rmsnorm_kernel,
        grid=(m // tm,),
        in_specs=[pl.BlockSpec((tm, d), lambda i: (i, 0)),
                  pl.BlockSpec((1, d), lambda i: (0, 0))],
        out_specs=pl.BlockSpec((tm, d), lambda i: (i, 0)),
        out_shape=jax.ShapeDtypeStruct((m, d), out_dtype),
        compiler_params=_params(("parallel",)),
        name="rmsnorm",
    )(x, g.reshape(1, d))


PROJ_TM = 2048
PROJ_TN = 256
PROJ_TK = 1024
FFN_SLAB = 512
MM_TM = 1024
MM_TN = 512


def _dot_cast_chunks(x_ref, w_ref, k0, kw, rows=slice(None), keep=None):
    acc = None
    for c0 in range(0, kw, PROJ_TK):
        c1 = min(c0 + PROJ_TK, kw)
        wb = w_ref[0, k0 + c0:k0 + c1, :].astype(BF16)
        if keep is not None:
            keep[c0:c1, :] = wb
        part = jnp.dot(x_ref[rows, c0:c1], wb, preferred_element_type=F32)
        acc = part if acc is None else acc + part
    return acc


def _sample_rows(store, compute):
    @pl.when(pl.program_id(0) == 0)
    def _():
        store(compute())

    @pl.when(pl.program_id(0) != 0)
    def _():
        store(None)


def _proj_kernel(*refs, k_parts, has_res):
    nparts = len(k_parts)
    xp_refs = refs[:nparts]
    if has_res:
        xs_ref, w_ref, rp_ref, rs_ref, yp_ref, ys_ref = refs[nparts:]
    else:
        xs_ref, w_ref, yp_ref, ys_ref = refs[nparts:]

    acc = None
    k0 = 0
    for xp_ref, kw in zip(xp_refs, k_parts):
        part = _dot_cast_chunks(xp_ref, w_ref, k0, kw)
        acc = part if acc is None else acc + part
        k0 += kw
    if has_res:
        acc = acc + rp_ref[...]
    yp_ref[...] = acc.astype(yp_ref.dtype)

    def sample():
        ys = jnp.dot(xs_ref[...].astype(BF16), w_ref[0].astype(BF16), preferred_element_type=F32)
        return ys + rs_ref[...] if has_res else ys

    def store(ys):
        ys_ref[0] = jnp.zeros(ys_ref.shape[1:], F32) if ys is None else ys

    _sample_rows(store, sample)


def _proj(xp_parts, xs, w, layer, *, n_off=0, n=None, res=None, out_dtype=F32, name="proj"):
    mp = xp_parts[0].shape[0]
    k_parts = tuple(x.shape[1] for x in xp_parts)
    kdim = sum(k_parts)
    ms = xs.shape[0]
    n = w.shape[2] if n is None else n
    tm, tn = PROJ_TM, PROJ_TN
    assert mp % tm == 0 and n % tn == 0 and n_off % tn == 0 and w.shape[1] == kdim
    assert xs.shape[1] == kdim
    joff = n_off // tn
    in_specs = [pl.BlockSpec((tm, kw), lambda i, j: (i, 0)) for kw in k_parts]
    in_specs += [pl.BlockSpec((ms, kdim), lambda i, j: (0, 0)),
                 pl.BlockSpec((1, kdim, tn), lambda i, j: (layer, 0, j + joff))]
    args = list(xp_parts) + [xs, w]
    if res is not None:
        in_specs += [pl.BlockSpec((tm, tn), lambda i, j: (i, j)),
                     pl.BlockSpec((ms, tn), lambda i, j: (0, j))]
        args += list(res)
    yp, ys = pl.pallas_call(
        functools.partial(_proj_kernel, k_parts=k_parts, has_res=res is not None),
        grid=(mp // tm, n // tn),
        in_specs=in_specs,
        out_specs=[pl.BlockSpec((tm, tn), lambda i, j: (i, j)),
                   pl.BlockSpec((1, ms, tn), lambda i, j: (i, 0, j))],
        out_shape=[jax.ShapeDtypeStruct((mp, n), out_dtype),
                   jax.ShapeDtypeStruct((mp // tm, ms, n), F32)],
        compiler_params=_params(("parallel", "parallel")),
        name=name,
    )(*args)
    return yp, ys[0]


def _mm_kernel(x_ref, w_ref, *rest, nk, has_res):
    if has_res:
        r_ref, o_ref = rest
    else:
        (o_ref,) = rest
    acc = jnp.dot(x_ref[...].astype(BF16), w_ref[0].astype(BF16), preferred_element_type=F32)
    if nk == 1:
        if has_res:
            acc = acc + r_ref[...]
        o_ref[...] = acc.astype(o_ref.dtype)
    else:
        k = pl.program_id(2)

        @pl.when(k == 0)
        def _():
            o_ref[...] = (acc + r_ref[...]) if has_res else acc

        @pl.when(k > 0)
        def _():
            o_ref[...] += acc


def _mm(x, w, layer, *, tm, tn, tk=None, n_off=0, n=None, res=None, out_dtype=F32, name="matmul"):
    m, kdim = x.shape
    n = w.shape[2] if n is None else n
    tk = kdim if tk is None else tk
    nk = kdim // tk
    assert m % tm == 0 and n % tn == 0 and kdim % tk == 0 and n_off % tn == 0
    assert nk == 1 or out_dtype == F32
    joff = n_off // tn
    in_specs = [pl.BlockSpec((tm, tk), lambda i, j, k: (i, k)),
                pl.BlockSpec((1, tk, tn), lambda i, j, k: (layer, k, j + joff))]
    args = [x, w]
    if res is not None:
        in_specs.append(pl.BlockSpec((tm, tn), lambda i, j, k: (i, j)))
        args.append(res)
    return pl.pallas_call(
        functools.partial(_mm_kernel, nk=nk, has_res=res is not None),
        grid=(m // tm, n // tn, nk),
        in_specs=in_specs,
        out_specs=pl.BlockSpec((tm, tn), lambda i, j, k: (i, j)),
        out_shape=jax.ShapeDtypeStruct((m, n), out_dtype),
        compiler_params=_params(("parallel", "parallel", "arbitrary")),
        name=name,
    )(*args)


def _moba_prompt_kernel(q_ref, k_ref, v_ref, o_ref, *, nblk):
    blk = MOBA_BLOCK
    scale = 1.0 / math.sqrt(A_HEAD_DIM)
    k = k_ref[...]
    kb = k.astype(BF16)
    vb = v_ref[...].astype(BF16)
    kmean = jnp.mean(k.reshape(nblk, blk, A_HEAD_DIM), axis=1)
    krow = lax.broadcasted_iota(jnp.int32, (blk, blk), 0)
    qcol = lax.broadcasted_iota(jnp.int32, (blk, blk), 1)
    causal = krow <= qcol
    for qb in range(nblk):
        qf = q_ref[qb * blk:(qb + 1) * blk, :]
        qbf = (qf * scale).astype(BF16)
        sel = None
        if qb > MOBA_TOPK:
            gate = _dot_f32(kmean, qf, NT_DIMS)
            rows = [gate[n:n + 1, :] for n in range(qb)]
            sel = []
            for n in range(qb):
                rank = jnp.zeros((1, blk), jnp.int32)
                for n2 in range(qb):
                    if n2 == n:
                        continue
                    beats = (rows[n2] >= rows[n]) if n2 < n else (rows[n2] > rows[n])
                    rank = rank + beats.astype(jnp.int32)
                sel.append(rank < MOBA_TOPK)
        s_list = []
        for n in range(qb + 1):
            s = lax.dot_general(kb[n * blk:(n + 1) * blk], qbf, NT_DIMS,
                                preferred_element_type=F32)
            if n == qb:
                s = jnp.where(causal, s, NEG)
            elif sel is not None:
                s = jnp.where(sel[n], s, NEG)
            s_list.append(s)
        m = s_list[0].max(axis=0, keepdims=True)
        for s in s_list[1:]:
            m = jnp.maximum(m, s.max(axis=0, keepdims=True))
        l = jnp.zeros((1, blk), F32)
        ot = jnp.zeros((A_HEAD_DIM, blk), F32)
        for n, s in enumerate(s_list):
            p = jnp.exp(s - m)
            l = l + p.sum(axis=0, keepdims=True)
            ot = ot + lax.dot_general(vb[n * blk:(n + 1) * blk], p.astype(BF16), TN_DIMS,
                                      preferred_element_type=F32)
        o_ref[qb * blk:(qb + 1) * blk, :] = (ot / l).T.astype(o_ref.dtype)


def _moba_prompt(q, k, v, nb, t):
    hd = A_HEAD_DIM
    spec = lambda: pl.BlockSpec((t, hd), lambda b, h: (b, h))
    return pl.pallas_call(
        functools.partial(_moba_prompt_kernel, nblk=t // MOBA_BLOCK),
        grid=(nb, A_HEADS),
        in_specs=[spec(), spec(), spec()],
        out_specs=spec(),
        out_shape=jax.ShapeDtypeStruct((nb * t, A_WIDTH), BF16),
        compiler_params=_params(("parallel", "parallel")),
        name="moba_prompt",
    )(q, k, v)


PAGES_PER_BLOCK = MOBA_BLOCK // PAGE_SIZE
KMEAN_BLOCKS = 2


def _kmean_kernel(pt_ref, *refs):
    o_ref = refs[-1]
    for i in range(KMEAN_BLOCKS):
        s = jnp.zeros((A_HEADS, A_HEAD_DIM), F32)
        for pg in range(PAGES_PER_BLOCK):
            s = s + jnp.sum(refs[i * PAGES_PER_BLOCK + pg][0, 0], axis=0)
        o_ref[0, i] = s * (1.0 / MOBA_BLOCK)


def _moba_kmean(cache_k, pt_flat, j, nb, n_pages):
    nblk = n_pages // PAGES_PER_BLOCK
    per_step = KMEAN_BLOCKS * PAGES_PER_BLOCK
    page = lambda pg: pl.BlockSpec(
        (1, 1, PAGE_SIZE, A_HEADS, A_HEAD_DIM),
        lambda b, n, pt: (j, pt[b * n_pages + per_step * n + pg], 0, 0, 0))
    return pl.pallas_call(
        _kmean_kernel,
        grid_spec=pltpu.PrefetchScalarGridSpec(
            num_scalar_prefetch=1,
            grid=(nb, nblk // KMEAN_BLOCKS),
            in_specs=[page(pg) for pg in range(per_step)],
            out_specs=pl.BlockSpec((1, KMEAN_BLOCKS, A_HEADS, A_HEAD_DIM),
                                   lambda b, n, pt: (b, n, 0, 0)),
        ),
        out_shape=jax.ShapeDtypeStruct((nb, nblk, A_HEADS, A_HEAD_DIM), F32),
        compiler_params=_params(("parallel", "parallel")),
        name="moba_kmean",
    )(pt_flat, *([cache_k] * per_step))


def _moba_select_kernel(q_ref, km_ref, o_ref, *, ts, nblk):
    q = q_ref[0]
    km = km_ref[0]
    nq = ts * A_HEADS
    qrep = jnp.concatenate(
        [jnp.broadcast_to(q[t:t + 1], (A_HEADS, A_WIDTH)) for t in range(ts)], axis=0)
    rowh = lax.broadcasted_iota(jnp.int32, (nq, A_WIDTH), 0) % A_HEADS
    colh = lax.broadcasted_iota(jnp.int32, (nq, A_WIDTH), 1) // A_HEAD_DIM
    qbig = jnp.where(rowh == colh, qrep, 0.0)
    gate = _dot_f32(km, qbig, NT_DIMS)
    nidx = lax.broadcasted_iota(jnp.int32, (nblk, nq), 0)
    for r in range(MOBA_TOPK):
        m = gate.max(axis=0, keepdims=True)
        idx = jnp.min(jnp.where(gate == m, nidx, nblk), axis=0, keepdims=True)
        o_ref[0, r:r + 1, :] = idx
        gate = jnp.where(nidx == idx, -jnp.inf, gate)


def _moba_select(qs3, kmean):
    nb, ts, _ = qs3.shape
    nblk = kmean.shape[1]
    nq = ts * A_HEADS
    return pl.pallas_call(
        functools.partial(_moba_select_kernel, ts=ts, nblk=nblk),
        grid=(nb,),
        in_specs=[pl.BlockSpec((1, ts, A_WIDTH), lambda b: (b, 0, 0)),
                  pl.BlockSpec((1, nblk, A_WIDTH), lambda b: (b, 0, 0))],
        out_specs=pl.BlockSpec((1, MOBA_TOPK, nq), lambda b: (b, 0, 0)),
        out_shape=jax.ShapeDtypeStruct((nb, MOBA_TOPK, nq), jnp.int32),
        compiler_params=_params(("parallel",)),
        name="moba_select",
    )(qs3, kmean)


def _moba_sample_kernel(sel_ref, pt_ref, q_ref, kn_ref, vn_ref, ck_hbm, cv_hbm, o_ref,
                        kbuf, vbuf, sems, *, j, ts, n_pages):
    b = pl.program_id(0)
    h = pl.program_id(1)
    nblk = n_pages // PAGES_PER_BLOCK
    nq = ts * A_HEADS
    step = b * A_HEADS + h
    nsteps = pl.num_programs(0) * A_HEADS
    half = step % 2

    def page_copies(bi, hi, hf):
        copies = []
        for t in range(ts):
            for r in range(MOBA_TOPK):
                blkid = jnp.minimum(sel_ref[bi * (MOBA_TOPK * nq) + r * nq + t * A_HEADS + hi], nblk - 1)
                for pg in range(PAGES_PER_BLOCK):
                    page = pt_ref[bi * n_pages + blkid * PAGES_PER_BLOCK + pg]
                    slot = (t * MOBA_TOPK + r) * PAGES_PER_BLOCK + pg
                    copies.append(pltpu.make_async_copy(
                        ck_hbm.at[j, page, :, hi, :], kbuf.at[hf, slot], sems.at[hf]))
                    copies.append(pltpu.make_async_copy(
                        cv_hbm.at[j, page, :, hi, :], vbuf.at[hf, slot], sems.at[hf]))
        return copies

    @pl.when(step == 0)
    def _():
        for c in page_copies(b, h, half):
            c.start()

    @pl.when(step + 1 < nsteps)
    def _():
        nxt = step + 1
        for c in page_copies(nxt // A_HEADS, nxt % A_HEADS, 1 - half):
            c.start()

    for c in page_copies(b, h, half):
        c.wait()

    scale = 1.0 / math.sqrt(A_HEAD_DIM)
    npg = MOBA_TOPK * PAGES_PER_BLOCK
    trow = lax.broadcasted_iota(jnp.int32, (ts, 1), 0)
    for t in range(ts):
        q = q_ref[0, t:t + 1, :] * scale
        slots = range(t * npg, (t + 1) * npg)
        s_cols = [jnp.sum(kbuf[half, sl] * q, axis=1, keepdims=True) for sl in slots]
        s_own = jnp.sum(kn_ref[0] * q, axis=1, keepdims=True)
        s_own = jnp.where(trow <= t, s_own, NEG)
        m = s_own.max(axis=0, keepdims=True)
        for s in s_cols:
            m = jnp.maximum(m, s.max(axis=0, keepdims=True))
        p_own = jnp.exp(s_own - m)
        l = p_own.sum(axis=0, keepdims=True)
        o = jnp.sum(p_own * vn_ref[0], axis=0, keepdims=True)
        for s, sl in zip(s_cols, slots):
            p = jnp.exp(s - m)
            l = l + p.sum(axis=0, keepdims=True)
            o = o + jnp.sum(p * vbuf[half, sl], axis=0, keepdims=True)
        o_ref[0, t:t + 1, :] = o / l


def _moba_sample(qs3, ks3, vs3, cache_k, cache_v, sel_flat, pt_flat, j, n_pages):
    nb, ts, _ = qs3.shape
    hd = A_HEAD_DIM
    nslot = ts * MOBA_TOPK * PAGES_PER_BLOCK
    own = lambda: pl.BlockSpec((1, ts, hd), lambda b, h, sel, pt: (b, 0, h))
    hbm = lambda: pl.BlockSpec(memory_space=pl.ANY)
    return pl.pallas_call(
        functools.partial(_moba_sample_kernel, j=j, ts=ts, n_pages=n_pages),
        grid_spec=pltpu.PrefetchScalarGridSpec(
            num_scalar_prefetch=2,
            grid=(nb, A_HEADS),
            in_specs=[own(), own(), own(), hbm(), hbm()],
            out_specs=own(),
            scratch_shapes=[pltpu.VMEM((2, nslot, PAGE_SIZE, hd), F32),
                            pltpu.VMEM((2, nslot, PAGE_SIZE, hd), F32),
                            pltpu.SemaphoreType.DMA((2,))],
        ),
        out_shape=jax.ShapeDtypeStruct((nb, ts, A_WIDTH), F32),
        compiler_params=_params(("arbitrary", "arbitrary")),
        name="moba_sample",
    )(sel_flat, pt_flat, qs3, ks3, vs3, cache_k, cache_v)


def _layernorm_swish(y, g, b):
    mu = jnp.mean(y, axis=-1, keepdims=True)
    yc = y - mu
    var = jnp.mean(yc * yc, axis=-1, keepdims=True)
    yn = yc * lax.rsqrt(var + EPS) * g + b
    return yn * _sigmoid(yn)


CONV_HALO = 32
CONV_ROWS = 128
CONV_COLS = 128


def _convmod_prompt_kernel(ga_ref, gb_ref, cw_ref, cb_ref, lg_ref, lb_ref, c_ref, nb_ref,
                           ubuf, ybuf, wbuf, *, tt, nt):
    w = B_CONV_WIDTH
    ch = ubuf.shape[1]
    t = pl.program_id(1)

    @pl.when(t == 0)
    def _():
        ubuf[0:CONV_HALO, :] = jnp.zeros((CONV_HALO, ch), F32)

    ubuf[CONV_HALO:CONV_HALO + tt, :] = ga_ref[...] * _sigmoid(gb_ref[...])
    base = CONV_HALO - (w - 1)
    for c0 in range(0, ch, CONV_COLS):
        cols = slice(c0, c0 + CONV_COLS)
        for r0 in range(0, tt, CONV_ROWS):
            acc = jnp.broadcast_to(cb_ref[:, cols], (CONV_ROWS, CONV_COLS))
            for s in range(F32_SUBLANES):
                taps = range(s, w, F32_SUBLANES)
                start = r0 + base + s
                rows = CONV_ROWS + F32_SUBLANES * (len(taps) - 1)
                wbuf[s, 0:rows, :] = ubuf[start:start + rows, cols]
                for a, j in enumerate(taps):
                    acc = acc + cw_ref[j:j + 1, cols] * \
                        wbuf[s, a * F32_SUBLANES:a * F32_SUBLANES + CONV_ROWS, :]
            ybuf[r0:r0 + CONV_ROWS, cols] = acc
    c_ref[...] = _layernorm_swish(ybuf[...], lg_ref[...], lb_ref[...]).astype(c_ref.dtype)

    @pl.when(t == nt - 1)
    def _():
        nb_ref[0] = ubuf[CONV_HALO + tt - (w - 1):CONV_HALO + tt, :]

    ubuf[0:CONV_HALO, :] = ubuf[tt:tt + CONV_HALO, :]


def _convmod_prompt(glu, nb, t, cw, cb, lg, lb):
    ch = glu.shape[1] // 2
    tt = 256
    nt = t // tt
    vec = lambda: pl.BlockSpec((1, ch), lambda b, i: (0, 0))
    return pl.pallas_call(
        functools.partial(_convmod_prompt_kernel, tt=tt, nt=nt),
        grid=(nb, nt),
        in_specs=[pl.BlockSpec((tt, ch), lambda b, i: (b * nt + i, 0)),
                  pl.BlockSpec((tt, ch), lambda b, i: (b * nt + i, 1)),
                  pl.BlockSpec((B_CONV_WIDTH, ch), lambda b, i: (0, 0)),
                  vec(), vec(), vec()],
        out_specs=[pl.BlockSpec((tt, ch), lambda b, i: (b * nt + i, 0)),
                   pl.BlockSpec((1, B_CONV_WIDTH - 1, ch), lambda b, i: (b, 0, 0))],
        out_shape=[jax.ShapeDtypeStruct((nb * t, ch), BF16),
                   jax.ShapeDtypeStruct((nb, B_CONV_WIDTH - 1, ch), F32)],
        scratch_shapes=[pltpu.VMEM((CONV_HALO + tt, ch), F32), pltpu.VMEM((tt, ch), F32),
                        pltpu.VMEM((F32_SUBLANES, CONV_ROWS + CONV_HALO, CONV_COLS), F32)],
        compiler_params=_params(("parallel", "arbitrary")),
        name="convmod_prompt",
    )(glu, glu, cw, cb.reshape(1, ch), lg.reshape(1, ch), lb.reshape(1, ch))


def _convmod_sample_kernel(ga_ref, gb_ref, cache_ref, cw_ref, cb_ref, lg_ref, lb_ref,
                           c_ref, nb_ref, ubuf, *, ts):
    w = B_CONV_WIDTH
    ch = ubuf.shape[1]
    ubuf[0:w - 1, :] = cache_ref[0]
    ubuf[w - 1:w - 1 + ts, :] = ga_ref[0] * _sigmoid(gb_ref[0])
    acc = jnp.broadcast_to(cb_ref[...], (ts, ch))
    for j in range(w):
        acc = acc + cw_ref[j:j + 1, :] * ubuf[j:j + ts, :]
    c_ref[0] = _layernorm_swish(acc, lg_ref[...], lb_ref[...])
    nb_ref[0] = ubuf[ts:ts + w - 1, :]


def _convmod_sample(glu3, cache, cw, cb, lg, lb):
    nb, ts, ch2 = glu3.shape
    ch = ch2 // 2
    w = B_CONV_WIDTH
    vec = lambda: pl.BlockSpec((1, ch), lambda b: (0, 0))
    return pl.pallas_call(
        functools.partial(_convmod_sample_kernel, ts=ts),
        grid=(nb,),
        in_specs=[pl.BlockSpec((1, ts, ch), lambda b: (b, 0, 0)),
                  pl.BlockSpec((1, ts, ch), lambda b: (b, 0, 1)),
                  pl.BlockSpec((1, w - 1, ch), lambda b: (b, 0, 0)),
                  pl.BlockSpec((w, ch), lambda b: (0, 0)),
                  vec(), vec(), vec()],
        out_specs=[pl.BlockSpec((1, ts, ch), lambda b: (b, 0, 0)),
                   pl.BlockSpec((1, w - 1, ch), lambda b: (b, 0, 0))],
        out_shape=[jax.ShapeDtypeStruct((nb, ts, ch), F32),
                   jax.ShapeDtypeStruct((nb, w - 1, ch), F32)],
        scratch_shapes=[pltpu.VMEM((w - 1 + ts + 5, ch), F32)],
        compiler_params=_params(("parallel",)),
        name="convmod_sample",
    )(glu3, glu3, cache, cw, cb.reshape(1, ch), lg.reshape(1, ch), lb.reshape(1, ch))


HG_CHUNK = 128
HG_SUB = 16
HG_HEADS = 8
HG_MILD_DECAY = 80.0


def _hgrn_lower_bound(lbp, layer):
    e = jnp.exp(lbp - lbp.max(axis=0, keepdims=True))
    sm = e / e.sum(axis=0, keepdims=True)
    lb = jnp.zeros_like(sm[0:1])
    for i in range(1, layer + 1):
        lb = lb + sm[i:i + 1]
    return lb


def _hgrn_out(o, g, gn):
    ms = jnp.mean(o * o, axis=-1, keepdims=True)
    return (o * lax.rsqrt(ms + EPS) * gn) * (g * _sigmoid(g))


def _hgrn_intra_exact(q, kk, b, kpad, bpad, consts):
    cs, sub, kd = HG_CHUNK, HG_SUB, C_KEY_DIM
    _, rowk, _, lane, blockdiag, _ = consts
    bref_rows = jnp.concatenate(
        [jnp.zeros((sub, kd), F32)]
        + [jnp.broadcast_to(b[sub * i - 1:sub * i], (sub, kd)) for i in range(1, cs // sub)], axis=0)
    qe = (q * jnp.exp(b - bref_rows)).astype(BF16)
    parts = [jnp.zeros((sub, cs), F32)]
    for i in range(1, cs // sub):
        bref = b[sub * i - 1:sub * i]
        ke = jnp.where(rowk < sub * i, kk * jnp.exp(jnp.minimum(bref - b, 0.0)), 0.0).astype(BF16)
        parts.append(lax.dot_general(qe[sub * i:sub * (i + 1)], ke, NT_DIMS,
                                     preferred_element_type=F32))
    a = jnp.concatenate(parts, axis=0)

    band = jnp.zeros((cs, cs), F32)
    for d in range(sub):
        if d == 0:
            p = q * kk
        else:
            p = q * kpad[sub - d:sub - d + cs, :] * jnp.exp(b - bpad[sub - d:sub - d + cs, :])
        band = jnp.where(lane == (cs - d) % cs, p.sum(axis=1, keepdims=True), band)
    band = pltpu.roll(band, 0, 1, stride=1, stride_axis=0)
    return a + jnp.where(blockdiag, band, 0.0)


def _hgrn_head(h, exact, refs, consts):
    q_ref, i_ref, g_ref, gn_ref, o_ref, s_ref, kpad, bpad = refs
    cs, sub, kd = HG_CHUNK, HG_SUB, C_KEY_DIM
    _, _, eye, _, _, lower = consts
    cols = slice(h * kd, (h + 1) * kd)
    q = q_ref[:, cols]
    vb = i_ref[:, cols].astype(BF16)
    kk = kpad[h, sub:, :]
    b = bpad[h, sub:, :]
    s_prev = s_ref[0, h]
    qeb = (q * jnp.exp(b)).astype(BF16)
    if exact:
        a = _hgrn_intra_exact(q, kk, b, kpad.at[h], bpad.at[h], consts)
    else:
        kinv = (kk * jnp.exp(-b)).astype(BF16)
        a = jnp.where(lower, lax.dot_general(qeb, kinv, NT_DIMS, preferred_element_type=F32), 0.0)
    o = jnp.dot(qeb, s_prev.astype(BF16), preferred_element_type=F32)
    o = o + jnp.dot(a.astype(BF16), vb, preferred_element_type=F32)
    bl = b[cs - 1:cs]
    kd_mat = (kk * jnp.exp(bl - b)).astype(BF16)
    decay_col = jnp.sum(jnp.where(eye, jnp.broadcast_to(jnp.exp(bl), (kd, kd)), 0.0),
                        axis=1, keepdims=True)
    s_ref[0, h] = decay_col * s_prev + lax.dot_general(kd_mat, vb, TN_DIMS,
                                                       preferred_element_type=F32)
    o_ref[:, cols] = _hgrn_out(o, g_ref[:, cols], gn_ref[...]).astype(o_ref.dtype)


def _hgrn_prompt_kernel(q_ref, f_ref, i_ref, g_ref, lbp_ref, gn_ref, o_ref, s_ref,
                        kpad, bpad, *, layer):
    cs, sub, kd = HG_CHUNK, HG_SUB, C_KEY_DIM

    @pl.when(pl.program_id(2) == 0)
    def _():
        s_ref[...] = jnp.zeros_like(s_ref)

    rr = lax.broadcasted_iota(jnp.int32, (cs, cs), 0)
    cc = lax.broadcasted_iota(jnp.int32, (cs, cs), 1)
    consts = (jnp.where(cc <= rr, 1.0, 0.0).astype(BF16),
              lax.broadcasted_iota(jnp.int32, (cs, kd), 0),
              rr == cc,
              cc,
              (rr // sub == cc // sub) & (cc <= rr),
              cc <= rr)
    tril = consts[0]
    kpad[:, 0:sub, :] = jnp.zeros((HG_HEADS, sub, kd), F32)
    bpad[:, 0:sub, :] = jnp.zeros((HG_HEADS, sub, kd), F32)
    lbs = _hgrn_lower_bound(lbp_ref[...], layer)
    b_last_min = None
    for h in range(HG_HEADS):
        cols = slice(h * kd, (h + 1) * kd)
        lb = lbs[:, cols]
        fg = lb + (1.0 - lb) * _sigmoid(f_ref[:, cols])
        l1, l2, l3 = _split3(jnp.log(fg))
        cum = lambda x: jnp.dot(tril, x, preferred_element_type=F32)
        b = (cum(l3) + cum(l2)) + cum(l1)
        kpad[h, sub:, :] = 1.0 - fg
        bpad[h, sub:, :] = b
        bl = b[cs - 1:cs]
        b_last_min = bl if b_last_min is None else jnp.minimum(b_last_min, bl)
    mild = jnp.min(b_last_min) > -HG_MILD_DECAY
    refs = (q_ref, i_ref, g_ref, gn_ref, o_ref, s_ref, kpad, bpad)

    @pl.when(mild)
    def _():
        for h in range(HG_HEADS):
            _hgrn_head(h, False, refs, consts)

    @pl.when(jnp.logical_not(mild))
    def _():
        for h in range(HG_HEADS):
            _hgrn_head(h, True, refs, consts)


def _hgrn_prompt(z, nb, t, heads, lower_bounds, gnorm, layer):
    kd = C_KEY_DIM
    assert HG_CHUNK == kd and heads % HG_HEADS == 0
    nchunk = t // HG_CHUNK
    depth = lower_bounds.shape[0]
    hw = HG_HEADS * kd
    ng = heads // HG_HEADS
    col = lambda part: pl.BlockSpec((HG_CHUNK, hw), lambda b, h, c: (b * nchunk + c, part * ng + h))
    return pl.pallas_call(
        functools.partial(_hgrn_prompt_kernel, layer=layer),
        grid=(nb, ng, nchunk),
        in_specs=[col(0), col(1), col(2), col(3),
                  pl.BlockSpec((depth, hw), lambda b, h, c: (0, h)),
                  pl.BlockSpec((1, kd), lambda b, h, c: (0, 0))],
        out_specs=[pl.BlockSpec((HG_CHUNK, hw), lambda b, h, c: (b * nchunk + c, h)),
                   pl.BlockSpec((1, HG_HEADS, kd, kd), lambda b, h, c: (b, h, 0, 0))],
        out_shape=[jax.ShapeDtypeStruct((nb * t, heads * kd), BF16),
                   jax.ShapeDtypeStruct((nb, heads, kd, kd), F32)],
        scratch_shapes=[pltpu.VMEM((HG_HEADS, HG_SUB + HG_CHUNK, kd), F32),
                        pltpu.VMEM((HG_HEADS, HG_SUB + HG_CHUNK, kd), F32)],
        compiler_params=_params(("parallel", "parallel", "arbitrary")),
        name="hgrn_prompt",
    )(z, z, z, z, lower_bounds, gnorm.reshape(1, kd))


def _hgrn_sample_kernel(q_ref, f_ref, i_ref, g_ref, lbp_ref, gn_ref, s0_ref, o_ref, s_ref,
                        *, layer, ts):
    kd = C_KEY_DIM
    lbs = _hgrn_lower_bound(lbp_ref[...], layer)
    eye = lax.broadcasted_iota(jnp.int32, (kd, kd), 0) == lax.broadcasted_iota(jnp.int32, (kd, kd), 1)
    col = lambda x: jnp.sum(jnp.where(eye, jnp.broadcast_to(x, (kd, kd)), 0.0), axis=1, keepdims=True)
    for h in range(HG_HEADS):
        cols = slice(h * kd, (h + 1) * kd)
        lb = lbs[:, cols]
        fg = lb + (1.0 - lb) * _sigmoid(f_ref[0, :, cols])
        kk = 1.0 - fg
        q = q_ref[0, :, cols]
        v = i_ref[0, :, cols]
        g = g_ref[0, :, cols]
        s = s0_ref[0, h]
        for t in range(ts):
            s = col(fg[t:t + 1]) * s + col(kk[t:t + 1]) * v[t:t + 1]
            o = jnp.sum(col(q[t:t + 1]) * s, axis=0, keepdims=True)
            o_ref[0, t:t + 1, cols] = _hgrn_out(o, g[t:t + 1], gn_ref[...])
        s_ref[0, h] = s


def _hgrn_sample(zs3, heads, lower_bounds, gnorm, state, j):
    nb, ts, _ = zs3.shape
    kd = C_KEY_DIM
    depth = lower_bounds.shape[0]
    assert heads % HG_HEADS == 0
    hw = HG_HEADS * kd
    ng = heads // HG_HEADS
    st = state.reshape(-1, heads, kd, kd)
    col = lambda part: pl.BlockSpec((1, ts, hw), lambda b, h: (b, 0, part * ng + h))
    return pl.pallas_call(
        functools.partial(_hgrn_sample_kernel, layer=2 * j + 1, ts=ts),
        grid=(nb, ng),
        in_specs=[col(0), col(1), col(2), col(3),
                  pl.BlockSpec((depth, hw), lambda b, h: (0, h)),
                  pl.BlockSpec((1, kd), lambda b, h: (0, 0)),
                  pl.BlockSpec((1, HG_HEADS, kd, kd), lambda b, h: (j * nb + b, h, 0, 0))],
        out_specs=[pl.BlockSpec((1, ts, hw), lambda b, h: (b, 0, h)),
                   pl.BlockSpec((1, HG_HEADS, kd, kd), lambda b, h: (b, h, 0, 0))],
        out_shape=[jax.ShapeDtypeStruct((nb, ts, heads * kd), F32),
                   jax.ShapeDtypeStruct((nb, heads, kd, kd), F32)],
        compiler_params=_params(("parallel", "parallel")),
        name="hgrn_sample",
    )(zs3, zs3, zs3, zs3, lower_bounds, gnorm.reshape(1, kd), st)


def _softmax_attend(q, k, v):
    scale = 1.0 / math.sqrt(q.shape[-1])
    s = lax.dot_general(q.astype(BF16), k.astype(BF16), NT_DIMS, preferred_element_type=F32) * scale
    m = s.max(axis=1, keepdims=True)
    p = jnp.exp(s - m)
    l = p.sum(axis=1, keepdims=True)
    return jnp.dot(p.astype(BF16), v.astype(BF16), preferred_element_type=F32) / l


def _mem_attn_prompt_kernel(q_ref, k_ref, v_ref, o_ref):
    o_ref[...] = _softmax_attend(q_ref[...], k_ref[...], v_ref[...]).astype(o_ref.dtype)


def _mem_attn_prompt(q, mk, mv, nb, t, n_mem):
    w = q.shape[1]
    hd = w // MEM_HEADS
    tq = 512
    nt = t // tq
    return pl.pallas_call(
        _mem_attn_prompt_kernel,
        grid=(nb, MEM_HEADS, nt),
        in_specs=[pl.BlockSpec((tq, hd), lambda b, h, i: (b * nt + i, h)),
                  pl.BlockSpec((n_mem, hd), lambda b, h, i: (b, h)),
                  pl.BlockSpec((n_mem, hd), lambda b, h, i: (b, h))],
        out_specs=pl.BlockSpec((tq, hd), lambda b, h, i: (b * nt + i, h)),
        out_shape=jax.ShapeDtypeStruct((nb * t, w), BF16),
        compiler_params=_params(("parallel", "parallel", "parallel")),
        name="mem_attn_prompt",
    )(q, mk, mv)


def _mem_attn_sample_kernel(q_ref, k_ref, v_ref, o_ref):
    hd = k_ref.shape[-1]
    for h in range(MEM_HEADS):
        cols = slice(h * hd, (h + 1) * hd)
        o_ref[0, :, cols] = _softmax_attend(q_ref[0, :, cols], k_ref[0, 0, :, h, :], v_ref[0, 0, :, h, :])


def _mem_attn_sample(q3, cache_k, cache_v, layer):
    nb, ts, w = q3.shape
    n_mem = cache_k.shape[2]
    hd = w // MEM_HEADS
    kv = lambda: pl.BlockSpec((1, 1, n_mem, MEM_HEADS, hd), lambda b: (layer, b, 0, 0, 0))
    return pl.pallas_call(
        _mem_attn_sample_kernel,
        grid=(nb,),
        in_specs=[pl.BlockSpec((1, ts, w), lambda b: (b, 0, 0)), kv(), kv()],
        out_specs=pl.BlockSpec((1, ts, w), lambda b: (b, 0, 0)),
        out_shape=jax.ShapeDtypeStruct((nb, ts, w), F32),
        compiler_params=_params(("parallel",)),
        name="mem_attn_sample",
    )(q3, cache_k, cache_v)


def _ffn_in_kernel(*refs, tm, tiles_per_seq):
    if tiles_per_seq > 1:
        (xp_ref, xprev_ref, xs_ref, wg_ref, wu_ref, cw_ref, cb_ref,
         h_ref, tail_ref, gs_ref, us_ref, gbuf, wgb, wub) = refs
    else:
        (xp_ref, xs_ref, wg_ref, wu_ref, cw_ref, cb_ref,
         h_ref, tail_ref, gs_ref, us_ref, gbuf, wgb, wub) = refs
    halo = BF16_SUBLANES
    d = xp_ref.shape[1]
    tn = h_ref.shape[1]
    if tiles_per_seq > 1:
        gprev = jnp.dot(xprev_ref[...], wg_ref[0].astype(BF16), preferred_element_type=F32)
        gprev = jnp.where(pl.program_id(0) % tiles_per_seq == 0, 0.0, gprev)
    else:
        gprev = jnp.zeros((halo, tn), F32)
    gbuf[0:halo, :] = gprev
    cw = cw_ref[...]
    for r0 in range(0, tm, FFN_SLAB):
        rows = slice(r0, r0 + FFN_SLAB)
        if r0 == 0:
            gate = _dot_cast_chunks(xp_ref, wg_ref, 0, d, rows=rows, keep=wgb)
            up = _dot_cast_chunks(xp_ref, wu_ref, 0, d, rows=rows, keep=wub)
        else:
            gate = jnp.dot(xp_ref[rows, :], wgb[...], preferred_element_type=F32)
            up = jnp.dot(xp_ref[rows, :], wub[...], preferred_element_type=F32)
        gbuf[halo + r0:halo + r0 + FFN_SLAB, :] = gate
        gc = (cw[0:1] * gbuf[halo - 2 + r0:halo - 2 + r0 + FFN_SLAB, :]
              + cw[1:2] * gbuf[halo - 1 + r0:halo - 1 + r0 + FFN_SLAB, :]
              + cw[2:3] * gate + cb_ref[...])
        h_ref[rows, :] = (gc * _sigmoid(gc) * up).astype(h_ref.dtype)
    tail_ref[0] = gbuf[halo + tm - (FFN_CONV_WIDTH - 1):halo + tm, :]

    def sample():
        xs = xs_ref[...].astype(BF16)
        return (jnp.dot(xs, wgb[...], preferred_element_type=F32),
                jnp.dot(xs, wub[...], preferred_element_type=F32))

    def store(gu):
        zero = jnp.zeros(gs_ref.shape[1:], F32)
        gs_ref[0], us_ref[0] = (zero, zero) if gu is None else gu

    _sample_rows(store, sample)


def _ffn_in(hp, hs, w_in, layer, cw, cb, t, dff):
    m, d = hp.shape
    ms = hs.shape[0]
    tm, tn = PROJ_TM, PROJ_TN
    halo = BF16_SUBLANES
    assert t % tm == 0 and dff % tn == 0
    tiles_per_seq = t // tm
    ni, nj = m // tm, dff // tn
    in_specs = [pl.BlockSpec((tm, d), lambda i, j: (i, 0), pipeline_mode=pl.Buffered(1))]
    args = [hp]
    if tiles_per_seq > 1:
        in_specs.append(pl.BlockSpec((halo, d), lambda i, j: (jnp.maximum(i * (tm // halo) - 1, 0), 0)))
        args.append(hp)
    in_specs += [pl.BlockSpec((ms, d), lambda i, j: (0, 0)),
                 pl.BlockSpec((1, d, tn), lambda i, j: (layer, 0, j)),
                 pl.BlockSpec((1, d, tn), lambda i, j: (layer, 0, nj + j)),
                 pl.BlockSpec((FFN_CONV_WIDTH, tn), lambda i, j: (0, j)),
                 pl.BlockSpec((1, tn), lambda i, j: (0, j))]
    args += [hs, w_in, w_in, cw, cb.reshape(1, dff)]
    sample_out = lambda: pl.BlockSpec((1, ms, tn), lambda i, j: (i, 0, j))
    hidden, tails, gs, us = pl.pallas_call(
        functools.partial(_ffn_in_kernel, tm=tm, tiles_per_seq=tiles_per_seq),
        grid=(ni, nj),
        in_specs=in_specs,
        out_specs=[pl.BlockSpec((tm, tn), lambda i, j: (i, j)),
                   pl.BlockSpec((1, FFN_CONV_WIDTH - 1, tn), lambda i, j: (i, 0, j)),
                   sample_out(), sample_out()],
        out_shape=[jax.ShapeDtypeStruct((m, dff), BF16),
                   jax.ShapeDtypeStruct((ni, FFN_CONV_WIDTH - 1, dff), F32),
                   jax.ShapeDtypeStruct((ni, ms, dff), F32),
                   jax.ShapeDtypeStruct((ni, ms, dff), F32)],
        scratch_shapes=[pltpu.VMEM((halo + tm, tn), F32),
                        pltpu.VMEM((d, tn), BF16), pltpu.VMEM((d, tn), BF16)],
        compiler_params=_params(("parallel", "parallel")),
        name="ffn_in",
    )(*args)
    return hidden, tails[tiles_per_seq - 1::tiles_per_seq], gs[0], us[0]


def _ffn_act_sample_kernel(g_ref, u_ref, cache_ref, cw_ref, cb_ref, h_ref, nb_ref, *, ts):
    w = FFN_CONV_WIDTH
    rows = [cache_ref[0, 0, r:r + 1, :] for r in range(w - 1)] + \
           [g_ref[0, t:t + 1, :] for t in range(ts)]
    for t in range(ts):
        gc = cb_ref[...]
        for j in range(w):
            gc = gc + cw_ref[j:j + 1, :] * rows[t + j]
        h_ref[0, t:t + 1, :] = gc * _sigmoid(gc) * u_ref[0, t:t + 1, :]
    for r in range(w - 1):
        nb_ref[0, r:r + 1, :] = rows[ts + r]


def _ffn_act_sample(gs3, us3, cache, layer, cw, cb):
    nb, ts, dff = gs3.shape
    w = FFN_CONV_WIDTH
    row = lambda: pl.BlockSpec((1, ts, dff), lambda b: (b, 0, 0))
    return pl.pallas_call(
        functools.partial(_ffn_act_sample_kernel, ts=ts),
        grid=(nb,),
        in_specs=[row(), row(),
                  pl.BlockSpec((1, 1, w - 1, dff), lambda b: (layer, b, 0, 0)),
                  pl.BlockSpec((w, dff), lambda b: (0, 0)),
                  pl.BlockSpec((1, dff), lambda b: (0, 0))],
        out_specs=[row(), pl.BlockSpec((1, w - 1, dff), lambda b: (b, 0, 0))],
        out_shape=[jax.ShapeDtypeStruct((nb, ts, dff), F32),
                   jax.ShapeDtypeStruct((nb, w - 1, dff), F32)],
        compiler_params=_params(("parallel",)),
        name="ffn_act_sample",
    )(gs3, us3, cache, cw, cb.reshape(1, dff))


def kernel(x_prompt, x_sample, mem_prompt, cache_moba_k, cache_moba_v, page_table, cache_convmod,
           state_hgrn, cache_mem_k, cache_mem_v, cache_ffn_conv, g_mix, g_mem_q, g_mem_kv, g_ffn,
           g_final, w_in_a, w_out_a, conv_w_b, conv_b_b, ln_g_b, ln_b_b, w_in_c, w_out_c,
           lower_bounds, gnorm_c, w_mem_q, w_mem_kv, w_mem_o, w_ffn_in, ffn_conv_w, ffn_conv_b,
           w_ffn_out):
    bp, tp, d = x_prompt.shape
    bs, ts, _ = x_sample.shape
    depth = g_mix.shape[0]
    n_mem = mem_prompt.shape[1]
    n_pages = page_table.shape[1]
    dff = w_ffn_out.shape[1]
    b_ch = conv_w_b.shape[2]
    c_heads = w_out_c.shape[1] // C_KEY_DIM
    mw = w_mem_q.shape[2]
    mp, ms = bp * tp, bs * ts

    xp = x_prompt.reshape(mp, d)
    xs = x_sample.reshape(ms, d)
    pt_flat = page_table.reshape(-1)
    mem2 = mem_prompt.reshape(bp * n_mem, d)
    w_ffn_out_bf = w_ffn_out.astype(BF16)

    outs = {k: [] for k in ("mk_p", "mv_p", "mk_s", "mv_s", "cv_p", "cv_s", "hg_p", "hg_s",
                            "memk", "memv", "ff_p", "ff_s")}
    for l in range(depth):
        j = l // 2
        hp = _rmsnorm(xp, g_mix[l], BF16)
        hs = _rmsnorm(xs, g_mix[l], F32)
        if l % 2 == 0:
            qp, qs = _proj([hp], hs, w_in_a, j, n_off=0, n=A_WIDTH, name="in_a_q")
            kp, ks = _proj([hp], hs, w_in_a, j, n_off=A_WIDTH, n=A_WIDTH, name="in_a_k")
            vp, vs = _proj([hp], hs, w_in_a, j, n_off=2 * A_WIDTH, n=A_WIDTH, name="in_a_v")
            glu_p, glu_s = _proj([hp], hs, w_in_a, j, n_off=3 * A_WIDTH, n=2 * b_ch, name="in_a_glu")
            qs3 = qs.reshape(bs, ts, A_WIDTH)
            oa_p = _moba_prompt(qp, kp, vp, bp, tp)
            c_p, buf_p = _convmod_prompt(glu_p, bp, tp, conv_w_b[j], conv_b_b[j],
                                         ln_g_b[j], ln_b_b[j])
            kmean = _moba_kmean(cache_moba_k, pt_flat, j, bs, n_pages)
            sel = _moba_select(qs3, kmean.reshape(bs, -1, A_WIDTH))
            oa_s = _moba_sample(qs3, ks.reshape(bs, ts, A_WIDTH), vs.reshape(bs, ts, A_WIDTH),
                                cache_moba_k, cache_moba_v, sel.reshape(-1), pt_flat, j, n_pages)
            c_s, buf_s = _convmod_sample(glu_s.reshape(bs, ts, 2 * b_ch), cache_convmod[j],
                                         conv_w_b[j], conv_b_b[j], ln_g_b[j], ln_b_b[j])
            cat_s = jnp.concatenate([oa_s.reshape(ms, A_WIDTH), c_s.reshape(ms, b_ch)], axis=1)
            xp, xs = _proj([oa_p, c_p], cat_s, w_out_a, j, res=(xp, xs), name="out_a")
            outs["mk_p"].append(kp.reshape(bp, tp, A_HEADS, A_HEAD_DIM))
            outs["mv_p"].append(vp.reshape(bp, tp, A_HEADS, A_HEAD_DIM))
            outs["mk_s"].append(ks.reshape(bs, ts, A_HEADS, A_HEAD_DIM))
            outs["mv_s"].append(vs.reshape(bs, ts, A_HEADS, A_HEAD_DIM))
            outs["cv_p"].append(buf_p)
            outs["cv_s"].append(buf_s)
        else:
            zp, zs = _proj([hp], hs, w_in_c, j, name="in_c")
            og_p, st_p = _hgrn_prompt(zp, bp, tp, c_heads, lower_bounds, gnorm_c[j], l)
            og_s, st_s = _hgrn_sample(zs.reshape(bs, ts, -1), c_heads, lower_bounds, gnorm_c[j],
                                      state_hgrn, j)
            xp, xs = _proj([og_p], og_s.reshape(ms, -1), w_out_c, j, res=(xp, xs), name="out_c")
            outs["hg_p"].append(st_p)
            outs["hg_s"].append(st_s)

        hm = _rmsnorm(mem2, g_mem_kv[l], BF16)
        mk = _mm(hm, w_mem_kv, l, tm=bp * n_mem, tn=MM_TN, n_off=0, n=mw, name="mem_k")
        mv = _mm(hm, w_mem_kv, l, tm=bp * n_mem, tn=MM_TN, n_off=mw, n=mw, name="mem_v")
        hp = _rmsnorm(xp, g_mem_q[l], BF16)
        hs = _rmsnorm(xs, g_mem_q[l], F32)
        qp, qs = _proj([hp], hs, w_mem_q, l, out_dtype=BF16, name="mem_q")
        ap = _mem_attn_prompt(qp, mk, mv, bp, tp, n_mem)
        a_s = _mem_attn_sample(qs.reshape(bs, ts, mw), cache_mem_k, cache_mem_v, l)
        xp, xs = _proj([ap], a_s.reshape(ms, mw), w_mem_o, l, res=(xp, xs), name="mem_o")
        outs["memk"].append(mk.reshape(bp, n_mem, MEM_HEADS, mw // MEM_HEADS))
        outs["memv"].append(mv.reshape(bp, n_mem, MEM_HEADS, mw // MEM_HEADS))

        hp = _rmsnorm(xp, g_ffn[l], BF16)
        hs = _rmsnorm(xs, g_ffn[l], F32)
        hid_p, fb_p, gs, us = _ffn_in(hp, hs, w_ffn_in, l, ffn_conv_w[l], ffn_conv_b[l], tp, dff)
        hid_s, fb_s = _ffn_act_sample(gs.reshape(bs, ts, dff), us.reshape(bs, ts, dff),
                                      cache_ffn_conv, l, ffn_conv_w[l], ffn_conv_b[l])
        xp = _mm(hid_p, w_ffn_out_bf, l, tm=MM_TM, tn=MM_TN, tk=dff // 2, res=xp,
                 name="ffn_out_prompt")
        xs = _mm(hid_s.reshape(ms, dff), w_ffn_out_bf, l, tm=ms, tn=MM_TN, tk=dff // 2, res=xs,
                 name="ffn_out_sample")
        outs["ff_p"].append(fb_p)
        outs["ff_s"].append(fb_s)

    y_prompt = _rmsnorm(xp, g_final, F32).reshape(bp, tp, d)
    y_sample = _rmsnorm(xs, g_final, F32).reshape(bs, ts, d)
    st = jnp.stack
    return (y_prompt, y_sample, st(outs["mk_p"]), st(outs["mv_p"]), st(outs["mk_s"]), st(outs["mv_s"]),
            st(outs["cv_p"]), st(outs["cv_s"]), st(outs["hg_p"]), st(outs["hg_s"]),
            st(outs["memk"]), st(outs["memv"]), st(outs["ff_p"]), st(outs["ff_s"]))
```

```python
import functools
import math

import jax
import jax.numpy as jnp
from jax import lax
from jax.experimental import pallas as pl
from jax.experimental.pallas import tpu as pltpu

F32 = jnp.float32
BF16 = jnp.bfloat16

EPS = 1e-6
NEG = -1e30

PAGE_SIZE = 128
A_HEADS = 16
A_HEAD_DIM = 128
A_WIDTH = A_HEADS * A_HEAD_DIM
MOBA_BLOCK = 256
MOBA_TOPK = 3
B_CONV_WIDTH = 31
C_KEY_DIM = 128
MEM_HEADS = 4
FFN_CONV_WIDTH = 3

LANES = 128
F32_SUBLANES = 8
BF16_SUBLANES = 16
VMEM_LIMIT = 60 * 1024 * 1024

NT_DIMS = (((1,), (1,)), ((), ()))
TN_DIMS = (((0,), (0,)), ((), ()))


def _params(sem, vmem=VMEM_LIMIT):
    return pltpu.CompilerParams(dimension_semantics=sem, vmem_limit_bytes=vmem)


def _sigmoid(x):
    return 1.0 / (1.0 + jnp.exp(-x))


def _split3(a):
    a1 = a.astype(BF16)
    r1 = a - a1.astype(F32)
    a2 = r1.astype(BF16)
    a3 = (r1 - a2.astype(F32)).astype(BF16)
    return a1, a2, a3


def _dot_f32(a, b, dims):
    a1, a2, a3 = _split3(a)
    b1, b2, b3 = _split3(b)
    d = lambda x, y: lax.dot_general(x, y, dims, preferred_element_type=F32)
    low = d(a1, b3) + d(a3, b1) + d(a2, b2)
    mid = d(a1, b2) + d(a2, b1)
    return (low + mid) + d(a1, b1)


def _rmsnorm_kernel(x_ref, g_ref, o_ref):
    x = x_ref[...]
    ms = jnp.mean(x * x, axis=-1, keepdims=True)
    o_ref[...] = (x * lax.rsqrt(ms + EPS) * g_ref[...]).astype(o_ref.dtype)


def _rmsnorm(x, g, out_dtype):
    m, d = x.shape
    tm = min(m, 512)
    return pl.pallas_call(
        _rmsnorm_kernel,
        grid=(m // tm,),
        in_specs=[pl.BlockSpec((tm, d), lambda i: (i, 0)),
                  pl.BlockSpec((1, d), lambda i: (0, 0))],
        out_specs=pl.BlockSpec((tm, d), lambda i: (i, 0)),
        out_shape=jax.ShapeDtypeStruct((m, d), out_dtype),
        compiler_params=_params(("parallel",)),
        name="rmsnorm",
    )(x, g.reshape(1, d))


PROJ_TM = 2048
PROJ_TN = 256
PROJ_TK = 1024
FFN_SLAB = 512
MM_TM = 1024
MM_TN = 512


def _dot_cast_chunks(x_ref, w_ref, k0, kw, rows=slice(None), keep=None):
    acc = None
    for c0 in range(0, kw, PROJ_TK):
        c1 = min(c0 + PROJ_TK, kw)
        wb = w_ref[0, k0 + c0:k0 + c1, :].astype(BF16)
        if keep is not None:
            keep[c0:c1, :] = wb
        part = jnp.dot(x_ref[rows, c0:c1], wb, preferred_element_type=F32)
        acc = part if acc is None else acc + part
    return acc


def _sample_rows(store, compute):
    @pl.when(pl.program_id(0) == 0)
    def _():
        store(compute())

    @pl.when(pl.program_id(0) != 0)
    def _():
        store(None)


def _proj_kernel(*refs, k_parts, has_res):
    nparts = len(k_parts)
    xp_refs = refs[:nparts]
    if has_res:
        xs_ref, w_ref, rp_ref, rs_ref, yp_ref, ys_ref = refs[nparts:]
    else:
        xs_ref, w_ref, yp_ref, ys_ref = refs[nparts:]

    acc = None
    k0 = 0
    for xp_ref, kw in zip(xp_refs, k_parts):
        part = _dot_cast_chunks(xp_ref, w_ref, k0, kw)
        acc = part if acc is None else acc + part
        k0 += kw
    if has_res:
        acc = acc + rp_ref[...]
    yp_ref[...] = acc.astype(yp_ref.dtype)

    def sample():
        ys = jnp.dot(xs_ref[...].astype(BF16), w_ref[0].astype(BF16), preferred_element_type=F32)
        return ys + rs_ref[...] if has_res else ys

    def store(ys):
        ys_ref[0] = jnp.zeros(ys_ref.shape[1:], F32) if ys is None else ys

    _sample_rows(store, sample)


def _proj(xp_parts, xs, w, layer, *, n_off=0, n=None, res=None, out_dtype=F32, name="proj"):
    mp = xp_parts[0].shape[0]
    k_parts = tuple(x.shape[1] for x in xp_parts)
    kdim = sum(k_parts)
    ms = xs.shape[0]
    n = w.shape[2] if n is None else n
    tm, tn = PROJ_TM, PROJ_TN
    assert mp % tm == 0 and n % tn == 0 and n_off % tn == 0 and w.shape[1] == kdim
    assert xs.shape[1] == kdim
    joff = n_off // tn
    in_specs = [pl.BlockSpec((tm, kw), lambda i, j: (i, 0)) for kw in k_parts]
    in_specs += [pl.BlockSpec((ms, kdim), lambda i, j: (0, 0)),
                 pl.BlockSpec((1, kdim, tn), lambda i, j: (layer, 0, j + joff))]
    args = list(xp_parts) + [xs, w]
    if res is not None:
        in_specs += [pl.BlockSpec((tm, tn), lambda i, j: (i, j)),
                     pl.BlockSpec((ms, tn), lambda i, j: (0, j))]
        args += list(res)
    yp, ys = pl.pallas_call(
        functools.partial(_proj_kernel, k_parts=k_parts, has_res=res is not None),
        grid=(mp // tm, n // tn),
        in_specs=in_specs,
        out_specs=[pl.BlockSpec((tm, tn), lambda i, j: (i, j)),
                   pl.BlockSpec((1, ms, tn), lambda i, j: (i, 0, j))],
        out_shape=[jax.ShapeDtypeStruct((mp, n), out_dtype),
                   jax.ShapeDtypeStruct((mp // tm, ms, n), F32)],
        compiler_params=_params(("parallel", "parallel")),
        name=name,
    )(*args)
    return yp, ys[0]


def _mm_kernel(x_ref, w_ref, *rest, nk, has_res):
    if has_res:
        r_ref, o_ref = rest
    else:
        (o_ref,) = rest
    acc = jnp.dot(x_ref[...].astype(BF16), w_ref[0].astype(BF16), preferred_element_type=F32)
    if nk == 1:
        if has_res:
            acc = acc + r_ref[...]
        o_ref[...] = acc.astype(o_ref.dtype)
    else:
        k = pl.program_id(2)

        @pl.when(k == 0)
        def _():
            o_ref[...] = (acc + r_ref[...]) if has_res else acc

        @pl.when(k > 0)
        def _():
            o_ref[...] += acc


def _mm(x, w, layer, *, tm, tn, tk=None, n_off=0, n=None, res=None, out_dtype=F32, name="matmul"):
    m, kdim = x.shape
    n = w.shape[2] if n is None else n
    tk = kdim if tk is None else tk
    nk = kdim // tk
    assert m % tm == 0 and n % tn == 0 and kdim % tk == 0 and n_off % tn == 0
    assert nk == 1 or out_dtype == F32
    joff = n_off // tn
    in_specs = [pl.BlockSpec((tm, tk), lambda i, j, k: (i, k)),
                pl.BlockSpec((1, tk, tn), lambda i, j, k: (layer, k, j + joff))]
    args = [x, w]
    if res is not None:
        in_specs.append(pl.BlockSpec((tm, tn), lambda i, j, k: (i, j)))
        args.append(res)
    return pl.pallas_call(
        functools.partial(_mm_kernel, nk=nk, has_res=res is not None),
        grid=(m // tm, n // tn, nk),
        in_specs=in_specs,
        out_specs=pl.BlockSpec((tm, tn), lambda i, j, k: (i, j)),
        out_shape=jax.ShapeDtypeStruct((m, n), out_dtype),
        compiler_params=_params(("parallel", "parallel", "arbitrary")),
        name=name,
    )(*args)


def _moba_prompt_kernel(q_ref, k_ref, v_ref, o_ref, *, nblk):
    blk = MOBA_BLOCK
    scale = 1.0 / math.sqrt(A_HEAD_DIM)
    k = k_ref[...]
    kb = k.astype(BF16)
    vb = v_ref[...].astype(BF16)
    kmean = jnp.mean(k.reshape(nblk, blk, A_HEAD_DIM), axis=1)
    krow = lax.broadcasted_iota(jnp.int32, (blk, blk), 0)
    qcol = lax.broadcasted_iota(jnp.int32, (blk, blk), 1)
    causal = krow <= qcol
    for qb in range(nblk):
        qf = q_ref[qb * blk:(qb + 1) * blk, :]
        qbf = (qf * scale).astype(BF16)
        sel = None
        if qb > MOBA_TOPK:
            gate = _dot_f32(kmean, qf, NT_DIMS)
            rows = [gate[n:n + 1, :] for n in range(qb)]
            sel = []
            for n in range(qb):
                rank = jnp.zeros((1, blk), jnp.int32)
                for n2 in range(qb):
                    if n2 == n:
                        continue
                    beats = (rows[n2] >= rows[n]) if n2 < n else (rows[n2] > rows[n])
                    rank = rank + beats.astype(jnp.int32)
                sel.append(rank < MOBA_TOPK)
        s_list = []
        for n in range(qb + 1):
            s = lax.dot_general(kb[n * blk:(n + 1) * blk], qbf, NT_DIMS,
                                preferred_element_type=F32)
            if n == qb:
                s = jnp.where(causal, s, NEG)
            elif sel is not None:
                s = jnp.where(sel[n], s, NEG)
            s_list.append(s)
        m = s_list[0].max(axis=0, keepdims=True)
        for s in s_list[1:]:
            m = jnp.maximum(m, s.max(axis=0, keepdims=True))
        l = jnp.zeros((1, blk), F32)
        ot = jnp.zeros((A_HEAD_DIM, blk), F32)
        for n, s in enumerate(s_list):
            p = jnp.exp(s - m)
            l = l + p.sum(axis=0, keepdims=True)
            ot = ot + lax.dot_general(vb[n * blk:(n + 1) * blk], p.astype(BF16), TN_DIMS,
                                      preferred_element_type=F32)
        o_ref[qb * blk:(qb + 1) * blk, :] = (ot / l).T.astype(o_ref.dtype)


def _moba_prompt(q, k, v, nb, t):
    hd = A_HEAD_DIM
    spec = lambda: pl.BlockSpec((t, hd), lambda b, h: (b, h))
    return pl.pallas_call(
        functools.partial(_moba_prompt_kernel, nblk=t // MOBA_BLOCK),
        grid=(nb, A_HEADS),
        in_specs=[spec(), spec(), spec()],
        out_specs=spec(),
        out_shape=jax.ShapeDtypeStruct((nb * t, A_WIDTH), BF16),
        compiler_params=_params(("parallel", "parallel")),
        name="moba_prompt",
    )(q, k, v)


PAGES_PER_BLOCK = MOBA_BLOCK // PAGE_SIZE
KMEAN_BLOCKS = 2


def _kmean_kernel(pt_ref, *refs):
    o_ref = refs[-1]
    for i in range(KMEAN_BLOCKS):
        s = jnp.zeros((A_HEADS, A_HEAD_DIM), F32)
        for pg in range(PAGES_PER_BLOCK):
            s = s + jnp.sum(refs[i * PAGES_PER_BLOCK + pg][0, 0], axis=0)
        o_ref[0, i] = s * (1.0 / MOBA_BLOCK)


def _moba_kmean(cache_k, pt_flat, j, nb, n_pages):
    nblk = n_pages // PAGES_PER_BLOCK
    per_step = KMEAN_BLOCKS * PAGES_PER_BLOCK
    page = lambda pg: pl.BlockSpec(
        (1, 1, PAGE_SIZE, A_HEADS, A_HEAD_DIM),
        lambda b, n, pt: (j, pt[b * n_pages + per_step * n + pg], 0, 0, 0))
    return pl.pallas_call(
        _kmean_kernel,
        grid_spec=pltpu.PrefetchScalarGridSpec(
            num_scalar_prefetch=1,
            grid=(nb, nblk // KMEAN_BLOCKS),
            in_specs=[page(pg) for pg in range(per_step)],
            out_specs=pl.BlockSpec((1, KMEAN_BLOCKS, A_HEADS, A_HEAD_DIM),
                                   lambda b, n, pt: (b, n, 0, 0)),
        ),
        out_shape=jax.ShapeDtypeStruct((nb, nblk, A_HEADS, A_HEAD_DIM), F32),
        compiler_params=_params(("parallel", "parallel")),
        name="moba_kmean",
    )(pt_flat, *([cache_k] * per_step))


def _moba_select_kernel(q_ref, km_ref, o_ref, *, ts, nblk):
    q = q_ref[0]
    km = km_ref[0]
    nq = ts * A_HEADS
    qrep = jnp.concatenate(
        [jnp.broadcast_to(q[t:t + 1], (A_HEADS, A_WIDTH)) for t in range(ts)], axis=0)
    rowh = lax.broadcasted_iota(jnp.int32, (nq, A_WIDTH), 0) % A_HEADS
    colh = lax.broadcasted_iota(jnp.int32, (nq, A_WIDTH), 1) // A_HEAD_DIM
    qbig = jnp.where(rowh == colh, qrep, 0.0)
    gate = _dot_f32(km, qbig, NT_DIMS)
    nidx = lax.broadcasted_iota(jnp.int32, (nblk, nq), 0)
    for r in range(MOBA_TOPK):
        m = gate.max(axis=0, keepdims=True)
        idx = jnp.min(jnp.where(gate == m, nidx, nblk), axis=0, keepdims=True)
        o_ref[0, r:r + 1, :] = idx
        gate = jnp.where(nidx == idx, -jnp.inf, gate)


def _moba_select(qs3, kmean):
    nb, ts, _ = qs3.shape
    nblk = kmean.shape[1]
    nq = ts * A_HEADS
    return pl.pallas_call(
        functools.partial(_moba_select_kernel, ts=ts, nblk=nblk),
        grid=(nb,),
        in_specs=[pl.BlockSpec((1, ts, A_WIDTH), lambda b: (b, 0, 0)),
                  pl.BlockSpec((1, nblk, A_WIDTH), lambda b: (b, 0, 0))],
        out_specs=pl.BlockSpec((1, MOBA_TOPK, nq), lambda b: (b, 0, 0)),
        out_shape=jax.ShapeDtypeStruct((nb, MOBA_TOPK, nq), jnp.int32),
        compiler_params=_params(("parallel",)),
        name="moba_select",
    )(qs3, kmean)


def _moba_sample_kernel(sel_ref, pt_ref, q_ref, kn_ref, vn_ref, ck_hbm, cv_hbm, o_ref,
                        kbuf, vbuf, sems, *, j, ts, n_pages):
    b = pl.program_id(0)
    h = pl.program_id(1)
    nblk = n_pages // PAGES_PER_BLOCK
    nq = ts * A_HEADS
    step = b * A_HEADS + h
    nsteps = pl.num_programs(0) * A_HEADS
    half = step % 2

    def page_copies(bi, hi, hf):
        copies = []
        for t in range(ts):
            for r in range(MOBA_TOPK):
                blkid = jnp.minimum(sel_ref[bi * (MOBA_TOPK * nq) + r * nq + t * A_HEADS + hi], nblk - 1)
                for pg in range(PAGES_PER_BLOCK):
                    page = pt_ref[bi * n_pages + blkid * PAGES_PER_BLOCK + pg]
                    slot = (t * MOBA_TOPK + r) * PAGES_PER_BLOCK + pg
                    copies.append(pltpu.make_async_copy(
                        ck_hbm.at[j, page, :, hi, :], kbuf.at[hf, slot], sems.at[hf]))
                    copies.append(pltpu.make_async_copy(
                        cv_hbm.at[j, page, :, hi, :], vbuf.at[hf, slot], sems.at[hf]))
        return copies

    @pl.when(step == 0)
    def _():
        for c in page_copies(b, h, half):
            c.start()

    @pl.when(step + 1 < nsteps)
    def _():
        nxt = step + 1
        for c in page_copies(nxt // A_HEADS, nxt % A_HEADS, 1 - half):
            c.start()

    for c in page_copies(b, h, half):
        c.wait()

    scale = 1.0 / math.sqrt(A_HEAD_DIM)
    npg = MOBA_TOPK * PAGES_PER_BLOCK
    trow = lax.broadcasted_iota(jnp.int32, (ts, 1), 0)
    for t in range(ts):
        q = q_ref[0, t:t + 1, :] * scale
        slots = range(t * npg, (t + 1) * npg)
        s_cols = [jnp.sum(kbuf[half, sl] * q, axis=1, keepdims=True) for sl in slots]
        s_own = jnp.sum(kn_ref[0] * q, axis=1, keepdims=True)
        s_own = jnp.where(trow <= t, s_own, NEG)
        m = s_own.max(axis=0, keepdims=True)
        for s in s_cols:
            m = jnp.maximum(m, s.max(axis=0, keepdims=True))
        p_own = jnp.exp(s_own - m)
        l = p_own.sum(axis=0, keepdims=True)
        o = jnp.sum(p_own * vn_ref[0], axis=0, keepdims=True)
        for s, sl in zip(s_cols, slots):
            p = jnp.exp(s - m)
            l = l + p.sum(axis=0, keepdims=True)
            o = o + jnp.sum(p * vbuf[half, sl], axis=0, keepdims=True)
        o_ref[0, t:t + 1, :] = o / l


def _moba_sample(qs3, ks3, vs3, cache_k, cache_v, sel_flat, pt_flat, j, n_pages):
    nb, ts, _ = qs3.shape
    hd = A_HEAD_DIM
    nslot = ts * MOBA_TOPK * PAGES_PER_BLOCK
    own = lambda: pl.BlockSpec((1, ts, hd), lambda b, h, sel, pt: (b, 0, h))
    hbm = lambda: pl.BlockSpec(memory_space=pl.ANY)
    return pl.pallas_call(
        functools.partial(_moba_sample_kernel, j=j, ts=ts, n_pages=n_pages),
        grid_spec=pltpu.PrefetchScalarGridSpec(
            num_scalar_prefetch=2,
            grid=(nb, A_HEADS),
            in_specs=[own(), own(), own(), hbm(), hbm()],
            out_specs=own(),
            scratch_shapes=[pltpu.VMEM((2, nslot, PAGE_SIZE, hd), F32),
                            pltpu.VMEM((2, nslot, PAGE_SIZE, hd), F32),
                            pltpu.SemaphoreType.DMA((2,))],
        ),
        out_shape=jax.ShapeDtypeStruct((nb, ts, A_WIDTH), F32),
        compiler_params=_params(("arbitrary", "arbitrary")),
        name="moba_sample",
    )(sel_flat, pt_flat, qs3, ks3, vs3, cache_k, cache_v)


def _layernorm_swish(y, g, b):
    mu = jnp.mean(y, axis=-1, keepdims=True)
    yc = y - mu
    var = jnp.mean(yc * yc, axis=-1, keepdims=True)
    yn = yc * lax.rsqrt(var + EPS) * g + b
    return yn * _sigmoid(yn)


CONV_HALO = 32
CONV_ROWS = 128
CONV_COLS = 128


def _convmod_prompt_kernel(ga_ref, gb_ref, cw_ref, cb_ref, lg_ref, lb_ref, c_ref, nb_ref,
                           ubuf, ybuf, wbuf, *, tt, nt):
    w = B_CONV_WIDTH
    ch = ubuf.shape[1]
    t = pl.program_id(1)

    @pl.when(t == 0)
    def _():
        ubuf[0:CONV_HALO, :] = jnp.zeros((CONV_HALO, ch), F32)

    ubuf[CONV_HALO:CONV_HALO + tt, :] = ga_ref[...] * _sigmoid(gb_ref[...])
    base = CONV_HALO - (w - 1)
    for c0 in range(0, ch, CONV_COLS):
        cols = slice(c0, c0 + CONV_COLS)
        for r0 in range(0, tt, CONV_ROWS):
            acc = jnp.broadcast_to(cb_ref[:, cols], (CONV_ROWS, CONV_COLS))
            for s in range(F32_SUBLANES):
                taps = range(s, w, F32_SUBLANES)
                start = r0 + base + s
                rows = CONV_ROWS + F32_SUBLANES * (len(taps) - 1)
                wbuf[s, 0:rows, :] = ubuf[start:start + rows, cols]
                for a, j in enumerate(taps):
                    acc = acc + cw_ref[j:j + 1, cols] * \
                        wbuf[s, a * F32_SUBLANES:a * F32_SUBLANES + CONV_ROWS, :]
            ybuf[r0:r0 + CONV_ROWS, cols] = acc
    c_ref[...] = _layernorm_swish(ybuf[...], lg_ref[...], lb_ref[...]).astype(c_ref.dtype)

    @pl.when(t == nt - 1)
    def _():
        nb_ref[0] = ubuf[CONV_HALO + tt - (w - 1):CONV_HALO + tt, :]

    ubuf[0:CONV_HALO, :] = ubuf[tt:tt + CONV_HALO, :]


def _convmod_prompt(glu, nb, t, cw, cb, lg, lb):
    ch = glu.shape[1] // 2
    tt = 256
    nt = t // tt
    vec = lambda: pl.BlockSpec((1, ch), lambda b, i: (0, 0))
    return pl.pallas_call(
        functools.partial(_convmod_prompt_kernel, tt=tt, nt=nt),
        grid=(nb, nt),
        in_specs=[pl.BlockSpec((tt, ch), lambda b, i: (b * nt + i, 0)),
                  pl.BlockSpec((tt, ch), lambda b, i: (b * nt + i, 1)),
                  pl.BlockSpec((B_CONV_WIDTH, ch), lambda b, i: (0, 0)),
                  vec(), vec(), vec()],
        out_specs=[pl.BlockSpec((tt, ch), lambda b, i: (b * nt + i, 0)),
                   pl.BlockSpec((1, B_CONV_WIDTH - 1, ch), lambda b, i: (b, 0, 0))],
        out_shape=[jax.ShapeDtypeStruct((nb * t, ch), BF16),
                   jax.ShapeDtypeStruct((nb, B_CONV_WIDTH - 1, ch), F32)],
        scratch_shapes=[pltpu.VMEM((CONV_HALO + tt, ch), F32), pltpu.VMEM((tt, ch), F32),
                        pltpu.VMEM((F32_SUBLANES, CONV_ROWS + CONV_HALO, CONV_COLS), F32)],
        compiler_params=_params(("parallel", "arbitrary")),
        name="convmod_prompt",
    )(glu, glu, cw, cb.reshape(1, ch), lg.reshape(1, ch), lb.reshape(1, ch))


def _convmod_sample_kernel(ga_ref, gb_ref, cache_ref, cw_ref, cb_ref, lg_ref, lb_ref,
                           c_ref, nb_ref, ubuf, *, ts):
    w = B_CONV_WIDTH
    ch = ubuf.shape[1]
    ubuf[0:w - 1, :] = cache_ref[0]
    ubuf[w - 1:w - 1 + ts, :] = ga_ref[0] * _sigmoid(gb_ref[0])
    acc = jnp.broadcast_to(cb_ref[...], (ts, ch))
    for j in range(w):
        acc = acc + cw_ref[j:j + 1, :] * ubuf[j:j + ts, :]
    c_ref[0] = _layernorm_swish(acc, lg_ref[...], lb_ref[...])
    nb_ref[0] = ubuf[ts:ts + w - 1, :]


def _convmod_sample(glu3, cache, cw, cb, lg, lb):
    nb, ts, ch2 = glu3.shape
    ch = ch2 // 2
    w = B_CONV_WIDTH
    vec = lambda: pl.BlockSpec((1, ch), lambda b: (0, 0))
    return pl.pallas_call(
        functools.partial(_convmod_sample_kernel, ts=ts),
        grid=(nb,),
        in_specs=[pl.BlockSpec((1, ts, ch), lambda b: (b, 0, 0)),
                  pl.BlockSpec((1, ts, ch), lambda b: (b, 0, 1)),
                  pl.BlockSpec((1, w - 1, ch), lambda b: (b, 0, 0)),
                  pl.BlockSpec((w, ch), lambda b: (0, 0)),
                  vec(), vec(), vec()],
        out_specs=[pl.BlockSpec((1, ts, ch), lambda b: (b, 0, 0)),
                   pl.BlockSpec((1, w - 1, ch), lambda b: (b, 0, 0))],
        out_shape=[jax.ShapeDtypeStruct((nb, ts, ch), F32),
                   jax.ShapeDtypeStruct((nb, w - 1, ch), F32)],
        scratch_shapes=[pltpu.VMEM((w - 1 + ts + 5, ch), F32)],
        compiler_params=_params(("parallel",)),
        name="convmod_sample",
    )(glu3, glu3, cache, cw, cb.reshape(1, ch), lg.reshape(1, ch), lb.reshape(1, ch))


HG_CHUNK = 128
HG_SUB = 16
HG_HEADS = 8
HG_MILD_DECAY = 60.0


def _hgrn_lower_bound(lbp, layer):
    e = jnp.exp(lbp - lbp.max(axis=0, keepdims=True))
    sm = e / e.sum(axis=0, keepdims=True)
    lb = jnp.zeros_like(sm[0:1])
    for i in range(1, layer + 1):
        lb = lb + sm[i:i + 1]
    return lb


def _hgrn_out(o, g, gn):
    ms = jnp.mean(o * o, axis=-1, keepdims=True)
    return (o * lax.rsqrt(ms + EPS) * gn) * (g * _sigmoid(g))


def _hgrn_intra_exact(q, kk, b, kpad, bpad, consts):
    cs, sub, kd = HG_CHUNK, HG_SUB, C_KEY_DIM
    _, rowk, _, lane, blockdiag, _ = consts
    bref_rows = jnp.concatenate(
        [jnp.zeros((sub, kd), F32)]
        + [jnp.broadcast_to(b[sub * i - 1:sub * i], (sub, kd)) for i in range(1, cs // sub)], axis=0)
    qe = (q * jnp.exp(b - bref_rows)).astype(BF16)
    parts = [jnp.zeros((sub, cs), F32)]
    for i in range(1, cs // sub):
        bref = b[sub * i - 1:sub * i]
        ke = jnp.where(rowk < sub * i, kk * jnp.exp(jnp.minimum(bref - b, 0.0)), 0.0).astype(BF16)
        parts.append(lax.dot_general(qe[sub * i:sub * (i + 1)], ke, NT_DIMS,
                                     preferred_element_type=F32))
    a = jnp.concatenate(parts, axis=0)

    band = jnp.zeros((cs, cs), F32)
    for d in range(sub):
        if d == 0:
            p = q * kk
        else:
            p = q * kpad[sub - d:sub - d + cs, :] * jnp.exp(b - bpad[sub - d:sub - d + cs, :])
        band = jnp.where(lane == (cs - d) % cs, p.sum(axis=1, keepdims=True), band)
    band = pltpu.roll(band, 0, 1, stride=1, stride_axis=0)
    return a + jnp.where(blockdiag, band, 0.0)


def _hgrn_head(h, exact, refs, consts):
    q_ref, i_ref, g_ref, gn_ref, o_ref, s_ref, kpad, bpad = refs
    cs, sub, kd = HG_CHUNK, HG_SUB, C_KEY_DIM
    _, _, eye, _, _, lower = consts
    cols = slice(h * kd, (h + 1) * kd)
    q = q_ref[:, cols]
    vb = i_ref[:, cols].astype(BF16)
    kk = kpad[h, sub:, :]
    b = bpad[h, sub:, :]
    s_prev = s_ref[0, h]
    qeb = (q * jnp.exp(b)).astype(BF16)
    if exact:
        a = _hgrn_intra_exact(q, kk, b, kpad.at[h], bpad.at[h], consts)
    else:
        kinv = (kk * jnp.exp(-b)).astype(BF16)
        a = jnp.where(lower, lax.dot_general(qeb, kinv, NT_DIMS, preferred_element_type=F32), 0.0)
    o = jnp.dot(qeb, s_prev.astype(BF16), preferred_element_type=F32)
    o = o + jnp.dot(a.astype(BF16), vb, preferred_element_type=F32)
    bl = b[cs - 1:cs]
    kd_mat = (kk * jnp.exp(bl - b)).astype(BF16)
    decay_col = jnp.sum(jnp.where(eye, jnp.broadcast_to(jnp.exp(bl), (kd, kd)), 0.0),
                        axis=1, keepdims=True)
    s_ref[0, h] = decay_col * s_prev + lax.dot_general(kd_mat, vb, TN_DIMS,
                                                       preferred_element_type=F32)
    o_ref[:, cols] = _hgrn_out(o, g_ref[:, cols], gn_ref[...]).astype(o_ref.dtype)


def _hgrn_prompt_kernel(q_ref, f_ref, i_ref, g_ref, lbp_ref, gn_ref, o_ref, s_ref,
                        kpad, bpad, *, layer):
    cs, sub, kd = HG_CHUNK, HG_SUB, C_KEY_DIM

    @pl.when(pl.program_id(2) == 0)
    def _():
        s_ref[...] = jnp.zeros_like(s_ref)

    rr = lax.broadcasted_iota(jnp.int32, (cs, cs), 0)
    cc = lax.broadcasted_iota(jnp.int32, (cs, cs), 1)
    consts = (jnp.where(cc <= rr, 1.0, 0.0).astype(BF16),
              lax.broadcasted_iota(jnp.int32, (cs, kd), 0),
              rr == cc,
              cc,
              (rr // sub == cc // sub) & (cc <= rr),
              cc <= rr)
    tril = consts[0]
    kpad[:, 0:sub, :] = jnp.zeros((HG_HEADS, sub, kd), F32)
    bpad[:, 0:sub, :] = jnp.zeros((HG_HEADS, sub, kd), F32)
    lbs = _hgrn_lower_bound(lbp_ref[...], layer)
    b_last_min = None
    for h in range(HG_HEADS):
        cols = slice(h * kd, (h + 1) * kd)
        lb = lbs[:, cols]
        fg = lb + (1.0 - lb) * _sigmoid(f_ref[:, cols])
        l1, l2, l3 = _split3(jnp.log(fg))
        cum = lambda x: jnp.dot(tril, x, preferred_element_type=F32)
        b = (cum(l3) + cum(l2)) + cum(l1)
        kpad[h, sub:, :] = 1.0 - fg
        bpad[h, sub:, :] = b
        bl = b[cs - 1:cs]
        b_last_min = bl if b_last_min is None else jnp.minimum(b_last_min, bl)
    mild = jnp.min(b_last_min) > -HG_MILD_DECAY
    refs = (q_ref, i_ref, g_ref, gn_ref, o_ref, s_ref, kpad, bpad)

    @pl.when(mild)
    def _():
        for h in range(HG_HEADS):
            _hgrn_head(h, False, refs, consts)

    @pl.when(jnp.logical_not(mild))
    def _():
        for h in range(HG_HEADS):
            _hgrn_head(h, True, refs, consts)


def _hgrn_prompt(z, nb, t, heads, lower_bounds, gnorm, layer):
    kd = C_KEY_DIM
    assert HG_CHUNK == kd and heads % HG_HEADS == 0
    nchunk = t // HG_CHUNK
    depth = lower_bounds.shape[0]
    hw = HG_HEADS * kd
    ng = heads // HG_HEADS
    col = lambda part: pl.BlockSpec((HG_CHUNK, hw), lambda b, h, c: (b * nchunk + c, part * ng + h))
    return pl.pallas_call(
        functools.partial(_hgrn_prompt_kernel, layer=layer),
        grid=(nb, ng, nchunk),
        in_specs=[col(0), col(1), col(2), col(3),
                  pl.BlockSpec((depth, hw), lambda b, h, c: (0, h)),
                  pl.BlockSpec((1, kd), lambda b, h, c: (0, 0))],
        out_specs=[pl.BlockSpec((HG_CHUNK, hw), lambda b, h, c: (b * nchunk + c, h)),
                   pl.BlockSpec((1, HG_HEADS, kd, kd), lambda b, h, c: (b, h, 0, 0))],
        out_shape=[jax.ShapeDtypeStruct((nb * t, heads * kd), BF16),
                   jax.ShapeDtypeStruct((nb, heads, kd, kd), F32)],
        scratch_shapes=[pltpu.VMEM((HG_HEADS, HG_SUB + HG_CHUNK, kd), F32),
                        pltpu.VMEM((HG_HEADS, HG_SUB + HG_CHUNK, kd), F32)],
        compiler_params=_params(("parallel", "parallel", "arbitrary")),
        name="hgrn_prompt",
    )(z, z, z, z, lower_bounds, gnorm.reshape(1, kd))


def _hgrn_sample_kernel(q_ref, f_ref, i_ref, g_ref, lbp_ref, gn_ref, s0_ref, o_ref, s_ref,
                        *, layer, ts):
    kd = C_KEY_DIM
    lbs = _hgrn_lower_bound(lbp_ref[...], layer)
    eye = lax.broadcasted_iota(jnp.int32, (kd, kd), 0) == lax.broadcasted_iota(jnp.int32, (kd, kd), 1)
    col = lambda x: jnp.sum(jnp.where(eye, jnp.broadcast_to(x, (kd, kd)), 0.0), axis=1, keepdims=True)
    for h in range(HG_HEADS):
        cols = slice(h * kd, (h + 1) * kd)
        lb = lbs[:, cols]
        fg = lb + (1.0 - lb) * _sigmoid(f_ref[0, :, cols])
        kk = 1.0 - fg
        q = q_ref[0, :, cols]
        v = i_ref[0, :, cols]
        g = g_ref[0, :, cols]
        s = s0_ref[0, h]
        for t in range(ts):
            s = col(fg[t:t + 1]) * s + col(kk[t:t + 1]) * v[t:t + 1]
            o = jnp.sum(col(q[t:t + 1]) * s, axis=0, keepdims=True)
            o_ref[0, t:t + 1, cols] = _hgrn_out(o, g[t:t + 1], gn_ref[...])
        s_ref[0, h] = s


def _hgrn_sample(zs3, heads, lower_bounds, gnorm, state, j):
    nb, ts, _ = zs3.shape
    kd = C_KEY_DIM
    depth = lower_bounds.shape[0]
    assert heads % HG_HEADS == 0
    hw = HG_HEADS * kd
    ng = heads // HG_HEADS
    st = state.reshape(-1, heads, kd, kd)
    col = lambda part: pl.BlockSpec((1, ts, hw), lambda b, h: (b, 0, part * ng + h))
    return pl.pallas_call(
        functools.partial(_hgrn_sample_kernel, layer=2 * j + 1, ts=ts),
        grid=(nb, ng),
        in_specs=[col(0), col(1), col(2), col(3),
                  pl.BlockSpec((depth, hw), lambda b, h: (0, h)),
                  pl.BlockSpec((1, kd), lambda b, h: (0, 0)),
                  pl.BlockSpec((1, HG_HEADS, kd, kd), lambda b, h: (j * nb + b, h, 0, 0))],
        out_specs=[pl.BlockSpec((1, ts, hw), lambda b, h: (b, 0, h)),
                   pl.BlockSpec((1, HG_HEADS, kd, kd), lambda b, h: (b, h, 0, 0))],
        out_shape=[jax.ShapeDtypeStruct((nb, ts, heads * kd), F32),
                   jax.ShapeDtypeStruct((nb, heads, kd, kd), F32)],
        compiler_params=_params(("parallel", "parallel")),
        name="hgrn_sample",
    )(zs3, zs3, zs3, zs3, lower_bounds, gnorm.reshape(1, kd), st)


def _softmax_attend(q, k, v):
    scale = 1.0 / math.sqrt(q.shape[-1])
    s = lax.dot_general(q.astype(BF16), k.astype(BF16), NT_DIMS, preferred_element_type=F32) * scale
    m = s.max(axis=1, keepdims=True)
    p = jnp.exp(s - m)
    l = p.sum(axis=1, keepdims=True)
    return jnp.dot(p.astype(BF16), v.astype(BF16), preferred_element_type=F32) / l


def _mem_attn_prompt_kernel(q_ref, k_ref, v_ref, o_ref):
    o_ref[...] = _softmax_attend(q_ref[...], k_ref[...], v_ref[...]).astype(o_ref.dtype)


def _mem_attn_prompt(q, mk, mv, nb, t, n_mem):
    w = q.shape[1]
    hd = w // MEM_HEADS
    tq = 512
    nt = t // tq
    return pl.pallas_call(
        _mem_attn_prompt_kernel,
        grid=(nb, MEM_HEADS, nt),
        in_specs=[pl.BlockSpec((tq, hd), lambda b, h, i: (b * nt + i, h)),
                  pl.BlockSpec((n_mem, hd), lambda b, h, i: (b, h)),
                  pl.BlockSpec((n_mem, hd), lambda b, h, i: (b, h))],
        out_specs=pl.BlockSpec((tq, hd), lambda b, h, i: (b * nt + i, h)),
        out_shape=jax.ShapeDtypeStruct((nb * t, w), BF16),
        compiler_params=_params(("parallel", "parallel", "parallel")),
        name="mem_attn_prompt",
    )(q, mk, mv)


def _mem_attn_sample_kernel(q_ref, k_ref, v_ref, o_ref):
    hd = k_ref.shape[-1]
    for h in range(MEM_HEADS):
        cols = slice(h * hd, (h + 1) * hd)
        o_ref[0, :, cols] = _softmax_attend(q_ref[0, :, cols], k_ref[0, 0, :, h, :], v_ref[0, 0, :, h, :])


def _mem_attn_sample(q3, cache_k, cache_v, layer):
    nb, ts, w = q3.shape
    n_mem = cache_k.shape[2]
    hd = w // MEM_HEADS
    kv = lambda: pl.BlockSpec((1, 1, n_mem, MEM_HEADS, hd), lambda b: (layer, b, 0, 0, 0))
    return pl.pallas_call(
        _mem_attn_sample_kernel,
        grid=(nb,),
        in_specs=[pl.BlockSpec((1, ts, w), lambda b: (b, 0, 0)), kv(), kv()],
        out_specs=pl.BlockSpec((1, ts, w), lambda b: (b, 0, 0)),
        out_shape=jax.ShapeDtypeStruct((nb, ts, w), F32),
        compiler_params=_params(("parallel",)),
        name="mem_attn_sample",
    )(q3, cache_k, cache_v)


def _ffn_in_kernel(*refs, tm, tiles_per_seq):
    if tiles_per_seq > 1:
        (xp_ref, xprev_ref, xs_ref, wg_ref, wu_ref, cw_ref, cb_ref,
         h_ref, tail_ref, gs_ref, us_ref, gbuf, wgb, wub) = refs
    else:
        (xp_ref, xs_ref, wg_ref, wu_ref, cw_ref, cb_ref,
         h_ref, tail_ref, gs_ref, us_ref, gbuf, wgb, wub) = refs
    halo = BF16_SUBLANES
    d = xp_ref.shape[1]
    tn = h_ref.shape[1]
    if tiles_per_seq > 1:
        gprev = jnp.dot(xprev_ref[...], wg_ref[0].astype(BF16), preferred_element_type=F32)
        gprev = jnp.where(pl.program_id(0) % tiles_per_seq == 0, 0.0, gprev)
    else:
        gprev = jnp.zeros((halo, tn), F32)
    gbuf[0:halo, :] = gprev
    cw = cw_ref[...]
    for r0 in range(0, tm, FFN_SLAB):
        rows = slice(r0, r0 + FFN_SLAB)
        if r0 == 0:
            gate = _dot_cast_chunks(xp_ref, wg_ref, 0, d, rows=rows, keep=wgb)
            up = _dot_cast_chunks(xp_ref, wu_ref, 0, d, rows=rows, keep=wub)
        else:
            gate = jnp.dot(xp_ref[rows, :], wgb[...], preferred_element_type=F32)
            up = jnp.dot(xp_ref[rows, :], wub[...], preferred_element_type=F32)
        gbuf[halo + r0:halo + r0 + FFN_SLAB, :] = gate
        gc = (cw[0:1] * gbuf[halo - 2 + r0:halo - 2 + r0 + FFN_SLAB, :]
              + cw[1:2] * gbuf[halo - 1 + r0:halo - 1 + r0 + FFN_SLAB, :]
              + cw[2:3] * gate + cb_ref[...])
        h_ref[rows, :] = (gc * _sigmoid(gc) * up).astype(h_ref.dtype)
    tail_ref[0] = gbuf[halo + tm - (FFN_CONV_WIDTH - 1):halo + tm, :]

    def sample():
        xs = xs_ref[...].astype(BF16)
        return (jnp.dot(xs, wgb[...], preferred_element_type=F32),
                jnp.dot(xs, wub[...], preferred_element_type=F32))

    def store(gu):
        zero = jnp.zeros(gs_ref.shape[1:], F32)
        gs_ref[0], us_ref[0] = (zero, zero) if gu is None else gu

    _sample_rows(store, sample)


def _ffn_in(hp, hs, w_in, layer, cw, cb, t, dff):
    m, d = hp.shape
    ms = hs.shape[0]
    tm, tn = PROJ_TM, PROJ_TN
    halo = BF16_SUBLANES
    assert t % tm == 0 and dff % tn == 0
    tiles_per_seq = t // tm
    ni, nj = m // tm, dff // tn
    in_specs = [pl.BlockSpec((tm, d), lambda i, j: (i, 0), pipeline_mode=pl.Buffered(1))]
    args = [hp]
    if tiles_per_seq > 1:
        in_specs.append(pl.BlockSpec((halo, d), lambda i, j: (jnp.maximum(i * (tm // halo) - 1, 0), 0)))
        args.append(hp)
    in_specs += [pl.BlockSpec((ms, d), lambda i, j: (0, 0)),
                 pl.BlockSpec((1, d, tn), lambda i, j: (layer, 0, j)),
                 pl.BlockSpec((1, d, tn), lambda i, j: (layer, 0, nj + j)),
                 pl.BlockSpec((FFN_CONV_WIDTH, tn), lambda i, j: (0, j)),
                 pl.BlockSpec((1, tn), lambda i, j: (0, j))]
    args += [hs, w_in, w_in, cw, cb.reshape(1, dff)]
    sample_out = lambda: pl.BlockSpec((1, ms, tn), lambda i, j: (i, 0, j))
    hidden, tails, gs, us = pl.pallas_call(
        functools.partial(_ffn_in_kernel, tm=tm, tiles_per_seq=tiles_per_seq),
        grid=(ni, nj),
        in_specs=in_specs,
        out_specs=[pl.BlockSpec((tm, tn), lambda i, j: (i, j)),
                   pl.BlockSpec((1, FFN_CONV_WIDTH - 1, tn), lambda i, j: (i, 0, j)),
                   sample_out(), sample_out()],
        out_shape=[jax.ShapeDtypeStruct((m, dff), BF16),
                   jax.ShapeDtypeStruct((ni, FFN_CONV_WIDTH - 1, dff), F32),
                   jax.ShapeDtypeStruct((ni, ms, dff), F32),
                   jax.ShapeDtypeStruct((ni, ms, dff), F32)],
        scratch_shapes=[pltpu.VMEM((halo + tm, tn), F32),
                        pltpu.VMEM((d, tn), BF16), pltpu.VMEM((d, tn), BF16)],
        compiler_params=_params(("parallel", "parallel")),
        name="ffn_in",
    )(*args)
    return hidden, tails[tiles_per_seq - 1::tiles_per_seq], gs[0], us[0]


def _ffn_act_sample_kernel(g_ref, u_ref, cache_ref, cw_ref, cb_ref, h_ref, nb_ref, *, ts):
    w = FFN_CONV_WIDTH
    rows = [cache_ref[0, 0, r:r + 1, :] for r in range(w - 1)] + \
           [g_ref[0, t:t + 1, :] for t in range(ts)]
    for t in range(ts):
        gc = cb_ref[...]
        for j in range(w):
            gc = gc + cw_ref[j:j + 1, :] * rows[t + j]
        h_ref[0, t:t + 1, :] = gc * _sigmoid(gc) * u_ref[0, t:t + 1, :]
    for r in range(w - 1):
        nb_ref[0, r:r + 1, :] = rows[ts + r]


def _ffn_act_sample(gs3, us3, cache, layer, cw, cb):
    nb, ts, dff = gs3.shape
    w = FFN_CONV_WIDTH
    row = lambda: pl.BlockSpec((1, ts, dff), lambda b: (b, 0, 0))
    return pl.pallas_call(
        functools.partial(_ffn_act_sample_kernel, ts=ts),
        grid=(nb,),
        in_specs=[row(), row(),
                  pl.BlockSpec((1, 1, w - 1, dff), lambda b: (layer, b, 0, 0)),
                  pl.BlockSpec((w, dff), lambda b: (0, 0)),
                  pl.BlockSpec((1, dff), lambda b: (0, 0))],
        out_specs=[row(), pl.BlockSpec((1, w - 1, dff), lambda b: (b, 0, 0))],
        out_shape=[jax.ShapeDtypeStruct((nb, ts, dff), F32),
                   jax.ShapeDtypeStruct((nb, w - 1, dff), F32)],
        compiler_params=_params(("parallel",)),
        name="ffn_act_sample",
    )(gs3, us3, cache, cw, cb.reshape(1, dff))


def kernel(x_prompt, x_sample, mem_prompt, cache_moba_k, cache_moba_v, page_table, cache_convmod,
           state_hgrn, cache_mem_k, cache_mem_v, cache_ffn_conv, g_mix, g_mem_q, g_mem_kv, g_ffn,
           g_final, w_in_a, w_out_a, conv_w_b, conv_b_b, ln_g_b, ln_b_b, w_in_c, w_out_c,
           lower_bounds, gnorm_c, w_mem_q, w_mem_kv, w_mem_o, w_ffn_in, ffn_conv_w, ffn_conv_b,
           w_ffn_out):
    bp, tp, d = x_prompt.shape
    bs, ts, _ = x_sample.shape
    depth = g_mix.shape[0]
    n_mem = mem_prompt.shape[1]
    n_pages = page_table.shape[1]
    dff = w_ffn_out.shape[1]
    b_ch = conv_w_b.shape[2]
    c_heads = w_out_c.shape[1] // C_KEY_DIM
    mw = w_mem_q.shape[2]
    mp, ms = bp * tp, bs * ts

    xp = x_prompt.reshape(mp, d)
    xs = x_sample.reshape(ms, d)
    pt_flat = page_table.reshape(-1)
    mem2 = mem_prompt.reshape(bp * n_mem, d)
    w_ffn_out_bf = w_ffn_out.astype(BF16)

    outs = {k: [] for k in ("mk_p", "mv_p", "mk_s", "mv_s", "cv_p", "cv_s", "hg_p", "hg_s",
                            "memk", "memv", "ff_p", "ff_s")}
    for l in range(depth):
        j = l // 2
        hp = _rmsnorm(xp, g_mix[l], BF16)
        hs = _rmsnorm(xs, g_mix[l], F32)
        if l % 2 == 0:
            qp, qs = _proj([hp], hs, w_in_a, j, n_off=0, n=A_WIDTH, name="in_a_q")
            kp, ks = _proj([hp], hs, w_in_a, j, n_off=A_WIDTH, n=A_WIDTH, name="in_a_k")
            vp, vs = _proj([hp], hs, w_in_a, j, n_off=2 * A_WIDTH, n=A_WIDTH, name="in_a_v")
            glu_p, glu_s = _proj([hp], hs, w_in_a, j, n_off=3 * A_WIDTH, n=2 * b_ch, name="in_a_glu")
            qs3 = qs.reshape(bs, ts, A_WIDTH)
            oa_p = _moba_prompt(qp, kp, vp, bp, tp)
            c_p, buf_p = _convmod_prompt(glu_p, bp, tp, conv_w_b[j], conv_b_b[j],
                                         ln_g_b[j], ln_b_b[j])
            kmean = _moba_kmean(cache_moba_k, pt_flat, j, bs, n_pages)
            sel = _moba_select(qs3, kmean.reshape(bs, -1, A_WIDTH))
            oa_s = _moba_sample(qs3, ks.reshape(bs, ts, A_WIDTH), vs.reshape(bs, ts, A_WIDTH),
                                cache_moba_k, cache_moba_v, sel.reshape(-1), pt_flat, j, n_pages)
            c_s, buf_s = _convmod_sample(glu_s.reshape(bs, ts, 2 * b_ch), cache_convmod[j],
                                         conv_w_b[j], conv_b_b[j], ln_g_b[j], ln_b_b[j])
            cat_s = jnp.concatenate([oa_s.reshape(ms, A_WIDTH), c_s.reshape(ms, b_ch)], axis=1)
            xp, xs = _proj([oa_p, c_p], cat_s, w_out_a, j, res=(xp, xs), name="out_a")
            outs["mk_p"].append(kp.reshape(bp, tp, A_HEADS, A_HEAD_DIM))
            outs["mv_p"].append(vp.reshape(bp, tp, A_HEADS, A_HEAD_DIM))
            outs["mk_s"].append(ks.reshape(bs, ts, A_HEADS, A_HEAD_DIM))
            outs["mv_s"].append(vs.reshape(bs, ts, A_HEADS, A_HEAD_DIM))
            outs["cv_p"].append(buf_p)
            outs["cv_s"].append(buf_s)
        else:
            zp, zs = _proj([hp], hs, w_in_c, j, name="in_c")
            og_p, st_p = _hgrn_prompt(zp, bp, tp, c_heads, lower_bounds, gnorm_c[j], l)
            og_s, st_s = _hgrn_sample(zs.reshape(bs, ts, -1), c_heads, lower_bounds, gnorm_c[j],
                                      state_hgrn, j)
            xp, xs = _proj([og_p], og_s.reshape(ms, -1), w_out_c, j, res=(xp, xs), name="out_c")
            outs["hg_p"].append(st_p)
            outs["hg_s"].append(st_s)

        hm = _rmsnorm(mem2, g_mem_kv[l], BF16)
        mk = _mm(hm, w_mem_kv, l, tm=bp * n_mem, tn=MM_TN, n_off=0, n=mw, name="mem_k")
        mv = _mm(hm, w_mem_kv, l, tm=bp * n_mem, tn=MM_TN, n_off=mw, n=mw, name="mem_v")
        hp = _rmsnorm(xp, g_mem_q[l], BF16)
        hs = _rmsnorm(xs, g_mem_q[l], F32)
        qp, qs = _proj([hp], hs, w_mem_q, l, out_dtype=BF16, name="mem_q")
        ap = _mem_attn_prompt(qp, mk, mv, bp, tp, n_mem)
        a_s = _mem_attn_sample(qs.reshape(bs, ts, mw), cache_mem_k, cache_mem_v, l)
        xp, xs = _proj([ap], a_s.reshape(ms, mw), w_mem_o, l, res=(xp, xs), name="mem_o")
        outs["memk"].append(mk.reshape(bp, n_mem, MEM_HEADS, mw // MEM_HEADS))
        outs["memv"].append(mv.reshape(bp, n_mem, MEM_HEADS, mw // MEM_HEADS))

        hp = _rmsnorm(xp, g_ffn[l], BF16)
        hs = _rmsnorm(xs, g_ffn[l], F32)
        hid_p, fb_p, gs, us = _ffn_in(hp, hs, w_ffn_in, l, ffn_conv_w[l], ffn_conv_b[l], tp, dff)
        hid_s, fb_s = _ffn_act_sample(gs.reshape(bs, ts, dff), us.reshape(bs, ts, dff),
                                      cache_ffn_conv, l, ffn_conv_w[l], ffn_conv_b[l])
        xp = _mm(hid_p, w_ffn_out_bf, l, tm=MM_TM, tn=MM_TN, tk=dff // 2, res=xp,
                 name="ffn_out_prompt")
        xs = _mm(hid_s.reshape(ms, dff), w_ffn_out_bf, l, tm=ms, tn=MM_TN, tk=dff // 2, res=xs,
                 name="ffn_out_sample")
        outs["ff_p"].append(fb_p)
        outs["ff_s"].append(fb_s)

    y_prompt = _rmsnorm(xp, g_final, F32).reshape(bp, tp, d)
    y_sample = _rmsnorm(xs, g_final, F32).reshape(bs, ts, d)
    st = jnp.stack
    return (y_prompt, y_sample, st(outs["mk_p"]), st(outs["mv_p"]), st(outs["mk_s"]), st(outs["mv_s"]),
            st(outs["cv_p"]), st(outs["cv_s"]), st(outs["hg_p"]), st(outs["hg_s"]),
            st(outs["memk"]), st(outs["memv"]), st(outs["ff_p"]), st(outs["ff_s"]))
```
